```python
import jax, jax.numpy as jnp
from jax import lax
import numpy as np

D_MODEL = 4096
BATCH = 1
SEQ = 8192
DEPTH = 4

N_A = DEPTH // 2
N_B = DEPTH - N_A
D_FF = -(-8 * D_MODEL // (3 * 256)) * 256
CONV_WIDTH = 31
HEAD_DIM = 128
N_HEADS = D_MODEL // HEAD_DIM
N_KV_GROUPS = 4
GROUP_SIZE = N_HEADS // N_KV_GROUPS
KV_WIDTH = N_KV_GROUPS * HEAD_DIM
CMP_LEN = 32
CMP_STRIDE = 16
CMP_HIDDEN = 4 * HEAD_DIM
SLC_BLOCK = 64
SLC_TOPK = 16
N_INIT_BLOCKS = 1
N_LOCAL_BLOCKS = 2
WINDOW = 512
Q_BLOCK = 128
ROPE_THETA = 10000.0
EPS = 1e-6
NEG_INF = -1e30

kernel_name = 'yoco_conformer_nsa_hybrid'


def rms_norm(x, g):
    xf = x.astype(jnp.float32)
    y = xf * lax.rsqrt(jnp.mean(xf * xf, axis=-1, keepdims=True) + EPS)
    return (y * g).astype(x.dtype)


def layer_norm(x, g, b):
    xf = x.astype(jnp.float32)
    mu = jnp.mean(xf, axis=-1, keepdims=True)
    var = jnp.mean(jnp.square(xf - mu), axis=-1, keepdims=True)
    return ((xf - mu) * lax.rsqrt(var + EPS) * g + b).astype(x.dtype)


def modulate(h, shift, scale):
    return h * (1.0 + scale[:, None, :]) + shift[:, None, :]


def rope(x, pos):
    half = HEAD_DIM // 2
    inv = ROPE_THETA ** (-jnp.arange(half, dtype=jnp.float32) / half)
    ang = pos.astype(jnp.float32)[:, None, :, None] * inv
    cos, sin = jnp.cos(ang), jnp.sin(ang)
    xf = x.astype(jnp.float32)
    x1, x2 = xf[..., :half], xf[..., half:]
    return jnp.concatenate([x1 * cos - x2 * sin, x2 * cos + x1 * sin], axis=-1).astype(x.dtype)


def swiglu(h, w_gu, w_down):
    a = h @ w_gu
    return (jax.nn.silu(a[..., :D_FF]) * a[..., D_FF:]) @ w_down


def conformer_conv(h, w_pw1, b_pw1, w_dw, b_dw, ln_g, ln_b, w_pw2, b_pw2):
    u = h @ w_pw1 + b_pw1
    u = u[..., :D_MODEL] * jax.nn.sigmoid(u[..., D_MODEL:])
    u = lax.conv_general_dilated(
        u, w_dw[:, None, :], window_strides=(1,), padding=((CONV_WIDTH - 1, 0),),
        dimension_numbers=('NWC', 'WIO', 'NWC'), feature_group_count=D_MODEL) + b_dw
    u = jax.nn.silu(layer_norm(u, ln_g, ln_b))
    return u @ w_pw2 + b_pw2


def _cmp_to_slc_matrix(n_cmp, n_slc):
    r, cl = SLC_BLOCK // CMP_STRIDE, CMP_LEN // CMP_STRIDE
    offs = (np.arange(r)[:, None] - np.arange(cl)[None, :]).reshape(-1)
    tgt = r * np.arange(n_slc)[None, :, None] + offs[None, None, :]
    return (np.arange(n_cmp)[:, None, None] == tgt).sum(-1).astype(np.float32)


def nsa_shared_kv(h, positions, kv_w, kv_k_norm, cmp_pos, cmp_w1, cmp_b1, cmp_w2, cmp_b2):
    B, S, _ = h.shape
    kv = (h @ kv_w).reshape(B, S, 6, N_KV_GROUPS, HEAD_DIM).transpose(2, 0, 3, 1, 4)
    n_cmp = (S - CMP_LEN) // CMP_STRIDE + 1
    starts = np.arange(n_cmp) * CMP_STRIDE
    idx = starts[:, None] + np.arange(CMP_LEN)[None, :]

    def compress(raw, j):
        blocks = raw[:, :, idx] + cmp_pos[j]
        flat = blocks.reshape(B, N_KV_GROUPS, n_cmp, CMP_LEN * HEAD_DIM)
        return jax.nn.silu(flat @ cmp_w1[j] + cmp_b1[j]) @ cmp_w2[j] + cmp_b2[j]

    pos_cmp = positions[:, starts + CMP_LEN - 1]
    k_cmp = rope(rms_norm(compress(kv[0], 0), kv_k_norm[0]), pos_cmp)
    v_cmp = compress(kv[1], 1)
    k_slc = rope(rms_norm(kv[2], kv_k_norm[1]), positions)
    k_win = rope(rms_norm(kv[4], kv_k_norm[2]), positions)
    return (k_cmp, v_cmp, k_slc, kv[3], k_win, kv[5])


def nsa_layer(h, positions, kv, w_q, q_norm_g, w_gate, b_gate, w_o):
    k_cmp, v_cmp, k_slc, v_slc, k_win, v_win = kv
    B, S, _ = h.shape
    G, R, QB, dh = N_KV_GROUPS, GROUP_SIZE, Q_BLOCK, HEAD_DIM
    n_qb = S // QB
    n_cmp = k_cmp.shape[2]
    n_slc = S // SLC_BLOCK
    k_top = min(SLC_TOPK, n_slc)

    q = (h @ w_q).reshape(B, S, N_HEADS, dh).transpose(0, 2, 1, 3)
    q = rope(rms_norm(q, q_norm_g), positions) * (dh ** -0.5)
    q_blocks = q.reshape(B, G, R, n_qb, QB, dh).transpose(3, 0, 1, 2, 4, 5)
    gates = jax.nn.sigmoid(h @ w_gate + b_gate)
    gates = gates.reshape(B, n_qb, QB, 3, G, R).transpose(1, 3, 0, 4, 5, 2)

    cmp_end = jnp.asarray(np.arange(n_cmp) * CMP_STRIDE + CMP_LEN - 1)
    cmp_to_slc = jnp.asarray(_cmp_to_slc_matrix(n_cmp, n_slc))
    k_slc_blk = k_slc.reshape(B, G, n_slc, SLC_BLOCK, dh)
    v_slc_blk = v_slc.reshape(B, G, n_slc, SLC_BLOCK, dh)
    pad = ((0, 0), (0, 0), (WINDOW, 0), (0, 0))
    k_win_pad, v_win_pad = jnp.pad(k_win, pad), jnp.pad(v_win, pad)
    b_ix = jnp.arange(B)[:, None, None, None]
    g_ix = jnp.arange(G)[None, :, None, None]
    blk_ids = jnp.arange(n_slc)

    def block_attend(args):
        qb, gb, blk = args
        t = blk * QB + jnp.arange(QB)
        valid_c = cmp_end[None, :] <= t[:, None]
        s_c = jnp.einsum('bgrqd,bgnd->bgrqn', qb, k_cmp).astype(jnp.float32)
        p_c = jax.nn.softmax(jnp.where(valid_c, s_c, NEG_INF), axis=-1) * valid_c
        o_c = jnp.einsum('bgrqn,bgnd->bgrqd', p_c.astype(v_cmp.dtype), v_cmp)
        imp = jnp.einsum('bgrqn,ns->bgqs', p_c, cmp_to_slc)
        dist = t[:, None] // SLC_BLOCK - blk_ids[None, :]
        forced = (blk_ids[None, :] < N_INIT_BLOCKS) | ((dist >= 0) & (dist < N_LOCAL_BLOCKS))
        imp = jnp.where(forced, jnp.inf, jnp.where(dist >= 0, imp, -jnp.inf))
        _, sel = lax.top_k(imp, k_top)
        k_sel = k_slc_blk[b_ix, g_ix, sel].reshape(B, G, QB, k_top * SLC_BLOCK, dh)
        v_sel = v_slc_blk[b_ix, g_ix, sel].reshape(B, G, QB, k_top * SLC_BLOCK, dh)
        kpos = (sel[..., None] * SLC_BLOCK + jnp.arange(SLC_BLOCK)).reshape(B, G, QB, k_top * SLC_BLOCK)
        valid_s = (kpos <= t[:, None])[:, :, None]
        s_s = jnp.einsum('bgrqd,bgqkd->bgrqk', qb, k_sel).astype(jnp.float32)
        p_s = jax.nn.softmax(jnp.where(valid_s, s_s, NEG_INF), axis=-1)
        o_s = jnp.einsum('bgrqk,bgqkd->bgrqd', p_s.astype(v_sel.dtype), v_sel)
        k_w = lax.dynamic_slice_in_dim(k_win_pad, blk * QB, WINDOW + QB, axis=2)
        v_w = lax.dynamic_slice_in_dim(v_win_pad, blk * QB, WINDOW + QB, axis=2)
        kpos_w = blk * QB - WINDOW + jnp.arange(WINDOW + QB)
        dist_w = t[:, None] - kpos_w[None, :]
        valid_w = (dist_w >= 0) & (dist_w < WINDOW) & (kpos_w[None, :] >= 0)
        s_w = jnp.einsum('bgrqd,bgkd->bgrqk', qb, k_w).astype(jnp.float32)
        p_w = jax.nn.softmax(jnp.where(valid_w, s_w, NEG_INF), axis=-1)
        o_w = jnp.einsum('bgrqk,bgkd->bgrqd', p_w.astype(v_w.dtype), v_w)
        return gb[0][..., None] * o_c + gb[1][..., None] * o_s + gb[2][..., None] * o_w

    out = lax.map(block_attend, (q_blocks, gates, jnp.arange(n_qb)))
    out = out.transpose(1, 0, 4, 2, 3, 5).reshape(B, S, N_HEADS * dh)
    return out @ w_o


def setup_inputs(seed: int = 0) -> dict:
    key = jax.random.key(seed)
    ks = jax.random.split(key, 32)
    f32 = jnp.float32
    D, F, dh = D_MODEL, D_FF, HEAD_DIM

    def nrm(k, shape, fan_in, scale=1.0):
        return jax.random.normal(k, shape, f32) * (scale * fan_in ** -0.5)

    def gain(k, shape):
        return 1.0 + 0.1 * jax.random.normal(k, shape, f32)

    def bias(k, shape):
        return 0.02 * jax.random.normal(k, shape, f32)

    gate_offset = jnp.asarray(np.array([0, 0, 1, 0, 0, 1], np.float32))[None, :, None]
    positions = (jax.random.randint(ks[2], (BATCH, 1), 0, 1024, jnp.int32)
                 + jnp.arange(SEQ, dtype=jnp.int32)[None, :])
    return {
        'x': jax.random.normal(ks[0], (BATCH, SEQ, D), f32),
        'c': jax.random.normal(ks[1], (BATCH, D), f32),
        'positions': positions,
        'w_ada': nrm(ks[3], (D, 6 * D), D, 0.5),
        'b_ada': bias(ks[4], (6 * D,)),
        'ada_emb': 0.1 * jax.random.normal(ks[5], (DEPTH, 6, D), f32) + gate_offset,
        'kv_ada_emb': 0.1 * jax.random.normal(ks[6], (2, D), f32),
        'norm_mix': gain(ks[7], (DEPTH, D)),
        'norm_ffn': gain(ks[8], (DEPTH, D)),
        'norm_kv': gain(ks[9], (D,)),
        'conv_w_pw1': nrm(ks[10], (N_A, D, 2 * D), D),
        'conv_b_pw1': bias(ks[11], (N_A, 2 * D)),
        'conv_w_dw': nrm(ks[12], (N_A, CONV_WIDTH, D), CONV_WIDTH),
        'conv_b_dw': bias(ks[13], (N_A, D)),
        'conv_ln_g': gain(ks[14], (N_A, D)),
        'conv_ln_b': bias(ks[15], (N_A, D)),
        'conv_w_pw2': nrm(ks[16], (N_A, D, D), D),
        'conv_b_pw2': bias(ks[17], (N_A, D)),
        'nsa_w_q': nrm(ks[18], (N_B, D, N_HEADS * dh), D),
        'nsa_q_norm': gain(ks[19], (N_B, dh)),
        'nsa_w_gate': nrm(ks[20], (N_B, D, 3 * N_HEADS), D),
        'nsa_b_gate': bias(ks[21], (N_B, 3 * N_HEADS)),
        'nsa_w_o': nrm(ks[22], (N_B, N_HEADS * dh, D), N_HEADS * dh),
        'kv_w': nrm(ks[23], (D, 6 * KV_WIDTH), D),
        'kv_k_norm': gain(ks[24], (3, dh)),
        'cmp_pos': 0.1 * jax.random.normal(ks[25], (2, CMP_LEN, dh), f32),
        'cmp_w1': nrm(ks[26], (2, CMP_LEN * dh, CMP_HIDDEN), CMP_LEN * dh),
        'cmp_b1': bias(ks[27], (2, CMP_HIDDEN)),
        'cmp_w2': nrm(ks[28], (2, CMP_HIDDEN, dh), CMP_HIDDEN),
        'cmp_b2': bias(ks[29], (2, dh)),
        'ffn_w_gu': nrm(ks[30], (DEPTH, D, 2 * F), D),
        'ffn_w_down': nrm(ks[31], (DEPTH, F, D), F),
    }


def reference(x, c, positions, w_ada, b_ada, ada_emb, kv_ada_emb, norm_mix, norm_ffn, norm_kv,
              conv_w_pw1, conv_b_pw1, conv_w_dw, conv_b_dw, conv_ln_g, conv_ln_b, conv_w_pw2, conv_b_pw2,
              nsa_w_q, nsa_q_norm, nsa_w_gate, nsa_b_gate, nsa_w_o,
              kv_w, kv_k_norm, cmp_pos, cmp_w1, cmp_b1, cmp_w2, cmp_b2,
              ffn_w_gu, ffn_w_down):
    B = x.shape[0]
    mod = (jax.nn.silu(c) @ w_ada + b_ada).reshape(B, 6, D_MODEL)
    kv = None
    for layer in range(DEPTH):
        m = mod + ada_emb[layer]
        h = modulate(rms_norm(x, norm_mix[layer]), m[:, 0], m[:, 1])
        if layer < N_A:
            y = conformer_conv(h, conv_w_pw1[layer], conv_b_pw1[layer], conv_w_dw[layer], conv_b_dw[layer],
                               conv_ln_g[layer], conv_ln_b[layer], conv_w_pw2[layer], conv_b_pw2[layer])
        else:
            if layer == N_A:
                h_kv = modulate(rms_norm(x, norm_kv), mod[:, 0] + kv_ada_emb[0], mod[:, 1] + kv_ada_emb[1])
                kv = nsa_shared_kv(h_kv, positions, kv_w, kv_k_norm, cmp_pos, cmp_w1, cmp_b1, cmp_w2, cmp_b2)
            i = layer - N_A
            y = nsa_layer(h, positions, kv, nsa_w_q[i], nsa_q_norm[i], nsa_w_gate[i], nsa_b_gate[i], nsa_w_o[i])
        x = x + m[:, 2][:, None, :] * y
        h = modulate(rms_norm(x, norm_ffn[layer]), m[:, 3], m[:, 4])
        x = x + m[:, 5][:, None, :] * swiglu(h, ffn_w_gu[layer], ffn_w_down[layer])
    return x
```

```python
import functools

import numpy as np
import jax
import jax.numpy as jnp
from jax import lax
from jax.experimental import pallas as pl
from jax.experimental.pallas import tpu as pltpu

HEAD_DIM = 128
N_KV_GROUPS = 4
CMP_LEN = 32
CMP_STRIDE = 16
SLC_BLOCK = 64
SLC_TOPK = 16
N_INIT_BLOCKS = 1
N_LOCAL_BLOCKS = 2
WINDOW = 512
ROPE_THETA = 10000.0
EPS = 1e-6
NEG_INF = -1e30

LANES = 128
SUBLANES = 8
VMEM_LIMIT_BYTES = 56 * 1024 * 1024

BF16 = jnp.bfloat16
F32 = jnp.float32


def _params(*sem):
    return pltpu.CompilerParams(dimension_semantics=sem, vmem_limit_bytes=VMEM_LIMIT_BYTES)


def _tile(n, target, quantum):
    if n <= target:
        return n
    t = (target // quantum) * quantum
    while t >= quantum:
        if n % t == 0:
            return t
        t -= quantum
    raise ValueError(f"no tile for {n} with quantum {quantum}")


def _dot(a, b):
    return jnp.dot(a, b, preferred_element_type=F32)


def _dot_nt(a, b):
    return lax.dot_general(a, b, (((1,), (1,)), ((), ())), preferred_element_type=F32)


def _sigmoid(v):
    return 1.0 / (1.0 + jnp.exp(-v))


def _silu(v):
    return v * _sigmoid(v)


def _ada_kernel(cb_ref, w_ref, b_ref, o_ref):
    d, tn = w_ref.shape
    n_chunks = tn // LANES

    def body(k, accs):
        r = pl.multiple_of(k * SUBLANES, SUBLANES)
        cv = _silu(cb_ref[pl.ds(r, SUBLANES), :])
        return tuple(
            accs[j] + cv * w_ref[pl.ds(r, SUBLANES), j * LANES:(j + 1) * LANES]
            for j in range(n_chunks))

    accs = lax.fori_loop(0, d // SUBLANES, body,
                         tuple(jnp.zeros((SUBLANES, LANES), F32) for _ in range(n_chunks)), unroll=4)
    for j in range(n_chunks):
        o_ref[:, j * LANES:(j + 1) * LANES] = (
            jnp.sum(accs[j], axis=0, keepdims=True) + b_ref[:, j * LANES:(j + 1) * LANES])


def _ada_proj(c, w_ada, b_ada):
    d, n = w_ada.shape
    tn = _tile(n, 512, LANES)
    cb = jnp.broadcast_to(c.reshape(d, 1), (d, LANES))
    return pl.pallas_call(
        _ada_kernel,
        grid=(n // tn,),
        in_specs=[pl.BlockSpec((d, LANES), lambda j: (0, 0)),
                  pl.BlockSpec((d, tn), lambda j: (0, j)),
                  pl.BlockSpec((1, tn), lambda j: (0, j))],
        out_specs=pl.BlockSpec((1, tn), lambda j: (0, j)),
        out_shape=jax.ShapeDtypeStruct((1, n), F32),
        compiler_params=_params("arbitrary"),
        name="ada_proj",
    )(cb, w_ada, b_ada.reshape(1, n))


def _rope_table_kernel(pos_ref, inv_ref, cos_ref, sin_ref):
    ang = pos_ref[...].astype(F32) * inv_ref[...]
    lane = lax.broadcasted_iota(jnp.int32, ang.shape, 1)
    cos_ref[...] = jnp.cos(ang)
    s = jnp.sin(ang)
    sin_ref[...] = jnp.where(lane < HEAD_DIM // 2, -s, s)


def _rope_tables(pos):
    n = pos.shape[0]
    half = HEAD_DIM // 2
    inv = ROPE_THETA ** (-jnp.arange(half, dtype=F32) / half)
    inv2 = jnp.concatenate([inv, inv]).reshape(1, HEAD_DIM)
    tm = _tile(n, 1024, SUBLANES)
    return pl.pallas_call(
        _rope_table_kernel,
        grid=(n // tm,),
        in_specs=[pl.BlockSpec((tm, 1), lambda i: (i, 0)),
                  pl.BlockSpec((1, HEAD_DIM), lambda i: (0, 0))],
        out_specs=[pl.BlockSpec((tm, HEAD_DIM), lambda i: (i, 0)),
                   pl.BlockSpec((tm, HEAD_DIM), lambda i: (i, 0))],
        out_shape=[jax.ShapeDtypeStruct((n, HEAD_DIM), F32)] * 2,
        compiler_params=_params("arbitrary"),
        name="rope_tables",
    )(pos.reshape(n, 1), inv2)


def _rms_rope(v, g, cos, sin):
    y = v * lax.rsqrt(jnp.mean(v * v, axis=-1, keepdims=True) + EPS) * g
    return y * cos + pltpu.roll(y, HEAD_DIM // 2, axis=1) * sin


def _prenorm_kernel(x_ref, g_ref, shift_ref, scale_ref, o_ref):
    x = x_ref[...]
    y = x * lax.rsqrt(jnp.mean(x * x, axis=-1, keepdims=True) + EPS) * g_ref[...]
    o_ref[...] = (y * (1.0 + scale_ref[...]) + shift_ref[...]).astype(o_ref.dtype)


def _prenorm(x, g, shift, scale):
    s, d = x.shape
    tm = _tile(s, 256, SUBLANES)
    vec = pl.BlockSpec((1, d), lambda i: (0, 0))
    return pl.pallas_call(
        _prenorm_kernel,
        grid=(s // tm,),
        in_specs=[pl.BlockSpec((tm, d), lambda i: (i, 0)), vec, vec, vec],
        out_specs=pl.BlockSpec((tm, d), lambda i: (i, 0)),
        out_shape=jax.ShapeDtypeStruct((s, d), BF16),
        compiler_params=_params("arbitrary"),
        name="prenorm",
    )(x, g.reshape(1, d), shift.reshape(1, d), scale.reshape(1, d))


def _mm_glu_kernel(a_ref, w1_ref, w2_ref, b1_ref, b2_ref, o_ref):
    a = a_ref[...]
    lin = _dot(a, w1_ref[...]) + b1_ref[...]
    gate = _dot(a, w2_ref[...]) + b2_ref[...]
    o_ref[...] = (lin * _sigmoid(gate)).astype(o_ref.dtype)


def _mm_glu(a, w, b):
    m, k = a.shape
    n = w.shape[1] // 2
    tm = _tile(m, 1024, SUBLANES)
    tn = _tile(n, 512, LANES)
    nj = n // tn
    b2d = b.reshape(1, 2 * n)
    return pl.pallas_call(
        _mm_glu_kernel,
        grid=(m // tm, nj),
        in_specs=[pl.BlockSpec((tm, k), lambda i, j: (i, 0)),
                  pl.BlockSpec((k, tn), lambda i, j: (0, j)),
                  pl.BlockSpec((k, tn), lambda i, j: (0, j + nj)),
                  pl.BlockSpec((1, tn), lambda i, j: (0, j)),
                  pl.BlockSpec((1, tn), lambda i, j: (0, j + nj))],
        out_specs=pl.BlockSpec((tm, tn), lambda i, j: (i, j)),
        out_shape=jax.ShapeDtypeStruct((m, n), F32),
        compiler_params=_params("arbitrary", "arbitrary"),
        name="mm_glu",
    )(a, w, w, b2d, b2d)


def _mm_swiglu_kernel(a_ref, wg_ref, wu_ref, o_ref):
    a = a_ref[...]
    o_ref[...] = (_silu(_dot(a, wg_ref[...])) * _dot(a, wu_ref[...])).astype(o_ref.dtype)


def _mm_swiglu(a, w):
    m, k = a.shape
    n = w.shape[1] // 2
    tm = _tile(m, 1024, SUBLANES)
    tn = _tile(n, 512, LANES)
    nj = n // tn
    return pl.pallas_call(
        _mm_swiglu_kernel,
        grid=(m // tm, nj),
        in_specs=[pl.BlockSpec((tm, k), lambda i, j: (i, 0)),
                  pl.BlockSpec((k, tn), lambda i, j: (0, j)),
                  pl.BlockSpec((k, tn), lambda i, j: (0, j + nj))],
        out_specs=pl.BlockSpec((tm, tn), lambda i, j: (i, j)),
        out_shape=jax.ShapeDtypeStruct((m, n), BF16),
        compiler_params=_params("arbitrary", "arbitrary"),
        name="mm_swiglu",
    )(a, w, w)


def _mm_resid_kernel(a_ref, w_ref, bias_ref, gate_ref, x_ref, o_ref):
    y = _dot(a_ref[...], w_ref[...]) + bias_ref[...]
    o_ref[...] = x_ref[...] + gate_ref[...] * y


def _mm_resid(a, w, bias, gate, x):
    m, k = a.shape
    n = w.shape[1]
    big_k = k > 8192
    tm = _tile(m, 512 if big_k else 1024, SUBLANES)
    tn = _tile(n, 256 if big_k else 512, LANES)
    vec = pl.BlockSpec((1, tn), lambda i, j: (0, j))
    return pl.pallas_call(
        _mm_resid_kernel,
        grid=(m // tm, n // tn),
        in_specs=[pl.BlockSpec((tm, k), lambda i, j: (i, 0)),
                  pl.BlockSpec((k, tn), lambda i, j: (0, j)),
                  vec, vec,
                  pl.BlockSpec((tm, tn), lambda i, j: (i, j))],
        out_specs=pl.BlockSpec((tm, tn), lambda i, j: (i, j)),
        out_shape=jax.ShapeDtypeStruct((m, n), F32),
        compiler_params=_params("arbitrary", "arbitrary"),
        name="mm_resid",
    )(a, w, bias.reshape(1, n), gate.reshape(1, n), x)


def _mm_sigmoid_kernel(a_ref, w_ref, b_ref, o_ref):
    o_ref[...] = _sigmoid(_dot(a_ref[...], w_ref[...]) + b_ref[...])


def _mm_sigmoid(a, w, b):
    m, k = a.shape
    n = w.shape[1]
    tm = _tile(m, 1024, SUBLANES)
    return pl.pallas_call(
        _mm_sigmoid_kernel,
        grid=(m // tm,),
        in_specs=[pl.BlockSpec((tm, k), lambda i: (i, 0)),
                  pl.BlockSpec((k, n), lambda i: (0, 0)),
                  pl.BlockSpec((1, n), lambda i: (0, 0))],
        out_specs=pl.BlockSpec((tm, n), lambda i: (i, 0)),
        out_shape=jax.ShapeDtypeStruct((m, n), F32),
        compiler_params=_params("arbitrary"),
        name="mm_sigmoid",
    )(a, w, b.reshape(1, n))


def _mm_heads_kernel(a_ref, w_ref, g_ref, cos_ref, sin_ref, o_ref, *, rope_every, out_scale):
    j = pl.program_id(1)
    nh = o_ref.shape[0]
    y = _dot(a_ref[...], w_ref[...])

    def plain():
        for h in range(nh):
            o_ref[h] = y[:, h * HEAD_DIM:(h + 1) * HEAD_DIM].astype(o_ref.dtype)

    def roped():
        g, cos, sin = g_ref[0], cos_ref[...], sin_ref[...]
        for h in range(nh):
            v = _rms_rope(y[:, h * HEAD_DIM:(h + 1) * HEAD_DIM], g, cos, sin)
            o_ref[h] = (v * out_scale).astype(o_ref.dtype)

    if rope_every == 0:
        plain()
    elif rope_every == 1:
        roped()
    else:
        pl.when(j % rope_every == 0)(roped)
        pl.when(j % rope_every != 0)(plain)


def _mm_heads(a, w, col0, n_cols, gains, gain_of_block, cos, sin, *, rope_every, out_scale, out_dtype):
    m, k = a.shape
    tn = 4 * HEAD_DIM
    nh = tn // HEAD_DIM
    tm = _tile(m, 1024, SUBLANES)
    j0 = col0 // tn
    ng = gains.shape[0]
    return pl.pallas_call(
        functools.partial(_mm_heads_kernel, rope_every=rope_every, out_scale=out_scale),
        grid=(m // tm, n_cols // tn),
        in_specs=[pl.BlockSpec((tm, k), lambda i, j: (i, 0)),
                  pl.BlockSpec((k, tn), lambda i, j: (0, j + j0)),
                  pl.BlockSpec((1, 1, HEAD_DIM), lambda i, j: (gain_of_block(j), 0, 0)),
                  pl.BlockSpec((tm, HEAD_DIM), lambda i, j: (i, 0)),
                  pl.BlockSpec((tm, HEAD_DIM), lambda i, j: (i, 0))],
        out_specs=pl.BlockSpec((nh, tm, HEAD_DIM), lambda i, j: (j, i, 0)),
        out_shape=jax.ShapeDtypeStruct((n_cols // HEAD_DIM, m, HEAD_DIM), out_dtype),
        compiler_params=_params("arbitrary", "arbitrary"),
        name="mm_heads",
    )(a, w, gains.reshape(ng, 1, HEAD_DIM), cos, sin)


CONV_ROWS = 64
CONV_HALO = 32


def _conv_kernel(halo_ref, u_ref, wdw_ref, bdw_ref, g_ref, b_ref, o_ref, buf_ref, acc_ref):
    i = pl.program_id(0)
    tm, d = u_ref.shape
    cw = wdw_ref.shape[0]
    buf_ref[pl.ds(CONV_HALO, tm), :] = u_ref[...]
    buf_ref[pl.ds(0, CONV_HALO), :] = jnp.where(i > 0, halo_ref[...], 0.0)
    off = CONV_HALO - (cw - 1)

    def col_body(c, carry):
        c0 = pl.multiple_of(c * LANES, LANES)
        acc = jnp.zeros((tm, LANES), F32)
        for w in range(cw):
            acc = acc + buf_ref[pl.ds(off + w, tm), pl.ds(c0, LANES)] * wdw_ref[pl.ds(w, 1), pl.ds(c0, LANES)]
        acc_ref[:, pl.ds(c0, LANES)] = acc + bdw_ref[:, pl.ds(c0, LANES)]
        return carry

    lax.fori_loop(0, d // LANES, col_body, 0)
    y = acc_ref[...]
    mu = jnp.mean(y, axis=-1, keepdims=True)
    yc = y - mu
    var = jnp.mean(yc * yc, axis=-1, keepdims=True)
    z = yc * lax.rsqrt(var + EPS) * g_ref[...] + b_ref[...]
    o_ref[...] = _silu(z).astype(o_ref.dtype)


def _conv_ln_silu(u, w_dw, b_dw, ln_g, ln_b):
    s, d = u.shape
    cw = w_dw.shape[0]
    assert cw - 1 <= CONV_HALO
    tm = _tile(s, CONV_ROWS, CONV_HALO)
    ratio = tm // CONV_HALO
    vec = pl.BlockSpec((1, d), lambda i: (0, 0))
    return pl.pallas_call(
        _conv_kernel,
        grid=(s // tm,),
        in_specs=[pl.BlockSpec((CONV_HALO, d), lambda i: (jnp.maximum(i * ratio - 1, 0), 0)),
                  pl.BlockSpec((tm, d), lambda i: (i, 0)),
                  pl.BlockSpec((cw, d), lambda i: (0, 0)),
                  vec, vec, vec],
        out_specs=pl.BlockSpec((tm, d), lambda i: (i, 0)),
        out_shape=jax.ShapeDtypeStruct((s, d), BF16),
        scratch_shapes=[pltpu.VMEM((CONV_HALO + tm, d), F32), pltpu.VMEM((tm, d), F32)],
        compiler_params=_params("arbitrary"),
        name="conv_ln_silu",
    )(u, u, w_dw, b_dw.reshape(1, d), ln_g.reshape(1, d), ln_b.reshape(1, d))


def _compress_kernel(c_ref, p_ref, w1_ref, b1_ref, w2_ref, b2_ref, g_ref, cos_ref, sin_ref, o_ref):
    j = pl.program_id(0)
    nch, half = c_ref.shape
    c = c_ref[...]
    a_lo = (c + p_ref[0:1, :]).astype(BF16)
    a_hi = (c + p_ref[1:2, :]).astype(BF16)
    h_lo = _dot(a_lo, w1_ref[pl.ds(0, half), :])
    h_hi = _dot(a_hi, w1_ref[pl.ds(half, half), :])
    hid = h_lo + pltpu.roll(h_hi, nch - 1, axis=0) + b1_ref[...]
    y = _dot(_silu(hid).astype(BF16), w2_ref[...]) + b2_ref[...]

    @pl.when(j == 0)
    def _():
        o_ref[...] = _rms_rope(y, g_ref[...], cos_ref[...], sin_ref[...]).astype(o_ref.dtype)

    @pl.when(j != 0)
    def _():
        o_ref[...] = y.astype(o_ref.dtype)


def _compress(raw, cmp_pos, w1, b1, w2, b2, g, cos_c, sin_c):
    _, ng, s, dh = raw.shape
    nch = s // CMP_STRIDE
    half = CMP_STRIDE * dh
    hid = w1.shape[-1]
    chunks = raw.reshape(2, ng, nch, half)
    pos2 = cmp_pos.reshape(2, CMP_LEN // CMP_STRIDE, half)
    sq = pl.Squeezed()
    return pl.pallas_call(
        _compress_kernel,
        grid=(2, ng),
        in_specs=[pl.BlockSpec((sq, sq, nch, half), lambda j, gi: (j, gi, 0, 0)),
                  pl.BlockSpec((sq, CMP_LEN // CMP_STRIDE, half), lambda j, gi: (j, 0, 0)),
                  pl.BlockSpec((sq, 2 * half, hid), lambda j, gi: (j, 0, 0)),
                  pl.BlockSpec((sq, 1, hid), lambda j, gi: (j, 0, 0)),
                  pl.BlockSpec((sq, hid, dh), lambda j, gi: (j, 0, 0)),
                  pl.BlockSpec((sq, 1, dh), lambda j, gi: (j, 0, 0)),
                  pl.BlockSpec((1, dh), lambda j, gi: (0, 0)),
                  pl.BlockSpec((nch, dh), lambda j, gi: (0, 0)),
                  pl.BlockSpec((nch, dh), lambda j, gi: (0, 0))],
        out_specs=pl.BlockSpec((sq, sq, nch, dh), lambda j, gi: (j, gi, 0, 0)),
        out_shape=jax.ShapeDtypeStruct((2, ng, nch, dh), BF16),
        compiler_params=_params("arbitrary", "arbitrary"),
        name="compress",
    )(chunks, pos2, w1, b1.reshape(2, 1, hid), w2, b2.reshape(2, 1, dh), g.reshape(1, dh), cos_c, sin_c)


NSA_TQ = 128
NSA_TK = 512


def _softmax_rows(s, valid):
    s = jnp.where(valid, s, NEG_INF)
    m = jnp.max(s, axis=-1, keepdims=True)
    e = jnp.where(valid, jnp.exp(s - m), 0.0)
    l = jnp.sum(e, axis=-1, keepdims=True)
    return e * jnp.where(l > 0.0, 1.0 / l, 0.0)


def _nsa_kernel(q_ref, kc_ref, vc_ref, ks_ref, vs_ref, kw_ref, vw_ref, gate_ref, mt_ref, o_ref,
                imp_ref, m_ref, l_ref, acc_ref):
    r, tq, dh = q_ref.shape
    nc = kc_ref.shape[0]
    ns = mt_ref.shape[0]
    s_len = ks_ref.shape[0]
    tk = min(NSA_TK, s_len)
    wk = min(WINDOW + tq, s_len)
    qt = pl.program_id(1)
    t0 = qt * tq
    q2 = q_ref[...].reshape(r * tq, dh)

    t_col = t0 + lax.broadcasted_iota(jnp.int32, (1, tq, 1), 1)
    cmp_end = lax.broadcasted_iota(jnp.int32, (1, 1, nc), 2) * CMP_STRIDE + (CMP_LEN - 1)
    s_c = _dot_nt(q2, kc_ref[...]).reshape(r, tq, nc)
    p_c = _softmax_rows(s_c, cmp_end <= t_col)
    o_c = _dot(p_c.reshape(r * tq, nc).astype(BF16), vc_ref[...])

    p_sum = jnp.sum(p_c, axis=0)
    p_hi = p_sum.astype(BF16)
    rem = p_sum - p_hi.astype(F32)
    p_mid = rem.astype(BF16)
    p_lo = (rem - p_mid.astype(F32)).astype(BF16)
    mt = mt_ref[...]
    imp = _dot_nt(mt, p_hi) + _dot_nt(mt, p_mid) + _dot_nt(mt, p_lo)

    t_row = t0 + lax.broadcasted_iota(jnp.int32, (1, tq), 1)
    blk = lax.broadcasted_iota(jnp.int32, (ns, 1), 0)
    dist = t_row // SLC_BLOCK - blk
    forced = (blk < N_INIT_BLOCKS) | ((dist >= 0) & (dist < N_LOCAL_BLOCKS))
    imp = jnp.where(forced, jnp.inf, jnp.where(dist >= 0, imp, -jnp.inf))
    imp_ref[...] = imp

    n_rivals = jnp.minimum((t0 + tq - 1) // SLC_BLOCK + 1, ns)

    def rank_body(i, cnt):
        row = imp_ref[pl.ds(i, 1), :]
        beats = (row > imp) | ((row == imp) & (i < blk))
        return cnt + jnp.where(beats, 1.0, 0.0)

    cnt = lax.fori_loop(0, n_rivals, rank_body, jnp.zeros((ns, tq), F32))
    sel = jnp.where(cnt < float(min(SLC_TOPK, ns)), 1.0, 0.0).T.astype(BF16)

    m_ref[...] = jnp.full(m_ref.shape, NEG_INF, F32)
    l_ref[...] = jnp.zeros(l_ref.shape, F32)
    acc_ref[...] = jnp.zeros(acc_ref.shape, F32)
    t_q = t0 + lax.broadcasted_iota(jnp.int32, (tq, 1), 0)
    n_tiles = (t0 + tq + tk - 1) // tk

    def sel_body(kt, carry):
        k0 = pl.multiple_of(kt * tk, tk)
        kpos = k0 + lax.broadcasted_iota(jnp.int32, (1, tk), 1)
        expand = (lax.broadcasted_iota(jnp.int32, (ns, 1), 0) == kpos // SLC_BLOCK)
        chosen = _dot(sel, jnp.where(expand, 1.0, 0.0).astype(BF16))
        mask = (chosen > 0.5) & (kpos <= t_q)
        s_s = _dot_nt(q2, ks_ref[pl.ds(k0, tk), :]).reshape(r, tq, tk)
        s_s = jnp.where(mask[None], s_s, NEG_INF)
        m_old = m_ref[...]
        m_new = jnp.maximum(m_old, jnp.max(s_s, axis=-1, keepdims=True))
        alpha = jnp.exp(m_old - m_new)
        p = jnp.exp(s_s - m_new)
        l_ref[...] = alpha * l_ref[...] + jnp.sum(p, axis=-1, keepdims=True)
        pv = _dot(p.reshape(r * tq, tk).astype(BF16), vs_ref[pl.ds(k0, tk), :])
        acc_ref[...] = alpha * acc_ref[...] + pv.reshape(r, tq, dh)
        m_ref[...] = m_new
        return carry

    lax.fori_loop(0, n_tiles, sel_body, 0)

    k_start = pl.multiple_of(jnp.maximum(jnp.minimum(t0 + tq, s_len) - wk, 0), SUBLANES * 2)
    back = t_col - (k_start + lax.broadcasted_iota(jnp.int32, (1, 1, wk), 2))
    s_w = _dot_nt(q2, kw_ref[pl.ds(k_start, wk), :]).reshape(r, tq, wk)
    p_w = _softmax_rows(s_w, (back >= 0) & (back < WINDOW))
    o_w = _dot(p_w.reshape(r * tq, wk).astype(BF16), vw_ref[pl.ds(k_start, wk), :])

    o_c = o_c.reshape(r, tq, dh)
    o_w = o_w.reshape(r, tq, dh)
    o_s = acc_ref[...] / l_ref[...]
    gates = gate_ref[...]
    for h in range(r):
        g_c = gates[:, h:h + 1]
        g_s = gates[:, r + h:r + h + 1]
        g_w = gates[:, 2 * r + h:2 * r + h + 1]
        o_ref[:, h * dh:(h + 1) * dh] = (g_c * o_c[h] + g_s * o_s[h] + g_w * o_w[h]).astype(o_ref.dtype)


def _cmp_to_slc_t(n_chunks, n_slc):
    n_cmp = n_chunks - 1
    r, cl = SLC_BLOCK // CMP_STRIDE, CMP_LEN // CMP_STRIDE
    offs = (np.arange(r)[:, None] - np.arange(cl)[None, :]).reshape(-1)
    tgt = r * np.arange(n_slc)[None, :, None] + offs[None, None, :]
    m = (np.arange(n_cmp)[:, None, None] == tgt).sum(-1).astype(np.float32)
    out = np.zeros((n_slc, n_chunks), np.float32)
    out[:, :n_cmp] = m.T
    return out


def _nsa_attention(q, k_cmp, v_cmp, k_slc, v_slc, k_win, v_win, gates):
    n_heads, s, dh = q.shape
    ng = k_slc.shape[0]
    r = n_heads // ng
    nc = k_cmp.shape[1]
    ns = s // SLC_BLOCK
    tq = _tile(s, NSA_TQ, SLC_BLOCK)
    mt = jnp.asarray(_cmp_to_slc_t(nc, ns), BF16)
    sq = pl.Squeezed()
    per_group = lambda n: pl.BlockSpec((sq, n, dh), lambda gi, qi: (gi, 0, 0))
    return pl.pallas_call(
        _nsa_kernel,
        grid=(ng, s // tq),
        in_specs=[pl.BlockSpec((r, tq, dh), lambda gi, qi: (gi, qi, 0)),
                  per_group(nc), per_group(nc), per_group(s), per_group(s), per_group(s), per_group(s),
                  pl.BlockSpec((sq, tq, 3 * r), lambda gi, qi: (gi, qi, 0)),
                  pl.BlockSpec((ns, nc), lambda gi, qi: (0, 0))],
        out_specs=pl.BlockSpec((tq, r * dh), lambda gi, qi: (qi, gi)),
        out_shape=jax.ShapeDtypeStruct((s, n_heads * dh), BF16),
        scratch_shapes=[pltpu.VMEM((ns, tq), F32),
                        pltpu.VMEM((r, tq, 1), F32), pltpu.VMEM((r, tq, 1), F32),
                        pltpu.VMEM((r, tq, dh), F32)],
        compiler_params=_params("arbitrary", "arbitrary"),
        name="nsa_attention",
    )(q, k_cmp, v_cmp, k_slc, v_slc, k_win, v_win, gates, mt)


def kernel(x, c, positions, w_ada, b_ada, ada_emb, kv_ada_emb, norm_mix, norm_ffn, norm_kv, conv_w_pw1, conv_b_pw1, conv_w_dw, conv_b_dw, conv_ln_g, conv_ln_b, conv_w_pw2, conv_b_pw2, nsa_w_q, nsa_q_norm, nsa_w_gate, nsa_b_gate, nsa_w_o, kv_w, kv_k_norm, cmp_pos, cmp_w1, cmp_b1, cmp_w2, cmp_b2, ffn_w_gu, ffn_w_down):
    batch, s, d = x.shape
    assert batch == 1, "the kernels are written for a single sequence"
    depth = ada_emb.shape[0]
    n_conv = conv_w_pw1.shape[0]
    ng = N_KV_GROUPS
    n_heads = d // HEAD_DIM
    r = n_heads // ng
    kvw = ng * HEAD_DIM
    zeros_d = jnp.zeros((d,), F32)

    xs = x[0]
    mod = _ada_proj(c, w_ada, b_ada).reshape(6, d)
    pos = positions[0]
    cos, sin = _rope_tables(pos)
    kv = None
    for layer in range(depth):
        m = mod + ada_emb[layer]
        h = _prenorm(xs, norm_mix[layer], m[0], m[1])
        if layer < n_conv:
            u = _mm_glu(h, conv_w_pw1[layer].astype(BF16), conv_b_pw1[layer])
            v = _conv_ln_silu(u, conv_w_dw[layer], conv_b_dw[layer], conv_ln_g[layer], conv_ln_b[layer])
            xs = _mm_resid(v, conv_w_pw2[layer].astype(BF16), conv_b_pw2[layer], m[2], xs)
        else:
            if kv is None:
                h_kv = _prenorm(xs, norm_kv, mod[0] + kv_ada_emb[0], mod[1] + kv_ada_emb[1])
                kv_wb = kv_w.astype(BF16)
                raw = _mm_heads(h_kv, kv_wb, 0, 2 * kvw, kv_k_norm, lambda j: 0, cos, sin,
                                rope_every=0, out_scale=1.0, out_dtype=F32)
                rest = _mm_heads(h_kv, kv_wb, 2 * kvw, 4 * kvw, kv_k_norm, lambda j: 1 + j // 2, cos, sin,
                                 rope_every=2, out_scale=1.0, out_dtype=BF16)
                rest = rest.reshape(4, ng, s, HEAD_DIM)
                n_chunks = s // CMP_STRIDE
                pos_c = jnp.pad(pos[CMP_LEN - 1::CMP_STRIDE], (0, 1))[:n_chunks]
                cos_c, sin_c = _rope_tables(pos_c)
                cmp = _compress(raw.reshape(2, ng, s, HEAD_DIM), cmp_pos, cmp_w1.astype(BF16), cmp_b1,
                                cmp_w2.astype(BF16), cmp_b2, kv_k_norm[0], cos_c, sin_c)
                kv = (cmp[0], cmp[1], rest[0], rest[1], rest[2], rest[3])
            i = layer - n_conv
            q = _mm_heads(h, nsa_w_q[i].astype(BF16), 0, n_heads * HEAD_DIM, nsa_q_norm[i:i + 1],
                          lambda j: 0, cos, sin, rope_every=1, out_scale=HEAD_DIM ** -0.5, out_dtype=BF16)
            gates = _mm_sigmoid(h, nsa_w_gate[i].astype(BF16), nsa_b_gate[i])
            gates = gates.reshape(s, 3, ng, r).transpose(2, 0, 1, 3).reshape(ng, s, 3 * r)
            o = _nsa_attention(q, *kv, gates)
            xs = _mm_resid(o, nsa_w_o[i].astype(BF16), zeros_d, m[2], xs)
        h = _prenorm(xs, norm_ffn[layer], m[3], m[4])
        a = _mm_swiglu(h, ffn_w_gu[layer].astype(BF16))
        xs = _mm_resid(a, ffn_w_down[layer].astype(BF16), zeros_d, m[5], xs)
    return xs[None]
```

```python
import functools
import math

import numpy as np
import jax
import jax.numpy as jnp
from jax import lax
from jax.experimental import pallas as pl
from jax.experimental.pallas import tpu as pltpu

HEAD_DIM = 128
N_KV_GROUPS = 4
CMP_LEN = 32
CMP_STRIDE = 16
SLC_BLOCK = 64
SLC_TOPK = 16
N_INIT_BLOCKS = 1
N_LOCAL_BLOCKS = 2
WINDOW = 512
ROPE_THETA = 10000.0
EPS = 1e-6
NEG_INF = -1e30
LOG2_E = math.log2(math.e)

LANES = 128
SUBLANES = 8
VMEM_LIMIT_BYTES = 56 * 1024 * 1024

BF16 = jnp.bfloat16
F32 = jnp.float32


def _params(*sem):
    return pltpu.CompilerParams(dimension_semantics=sem, vmem_limit_bytes=VMEM_LIMIT_BYTES)


def _tile(n, target, quantum):
    if n <= target:
        return n
    t = (target // quantum) * quantum
    while t >= quantum:
        if n % t == 0:
            return t
        t -= quantum
    raise ValueError(f"no tile for {n} with quantum {quantum}")


def _round_up(n, q):
    return -(-n // q) * q


def _dot(a, b):
    return jnp.dot(a, b, preferred_element_type=F32)


def _dot_nt(a, b):
    return lax.dot_general(a, b, (((1,), (1,)), ((), ())), preferred_element_type=F32)


def _sigmoid(v):
    return 1.0 / (1.0 + jnp.exp(-v))


def _silu(v):
    return v * _sigmoid(v)


def _lane_tile(v, width):
    reps = width // v.shape[1]
    return v if reps == 1 else jnp.concatenate([v] * reps, axis=1)


def _ada_kernel(cb_ref, w_ref, b_ref, o_ref):
    d, tn = w_ref.shape
    n_chunks = tn // LANES

    def body(k, accs):
        r = pl.multiple_of(k * SUBLANES, SUBLANES)
        cv = _silu(cb_ref[pl.ds(r, SUBLANES), :])
        return tuple(
            accs[j] + cv * w_ref[pl.ds(r, SUBLANES), j * LANES:(j + 1) * LANES]
            for j in range(n_chunks))

    accs = lax.fori_loop(0, d // SUBLANES, body,
                         tuple(jnp.zeros((SUBLANES, LANES), F32) for _ in range(n_chunks)), unroll=4)
    for j in range(n_chunks):
        o_ref[:, j * LANES:(j + 1) * LANES] = (
            jnp.sum(accs[j], axis=0, keepdims=True) + b_ref[:, j * LANES:(j + 1) * LANES])


def _ada_proj(c, w_ada, b_ada):
    d, n = w_ada.shape
    tn = _tile(n, 512, LANES)
    cb = jnp.broadcast_to(c.reshape(d, 1), (d, LANES))
    return pl.pallas_call(
        _ada_kernel,
        grid=(n // tn,),
        in_specs=[pl.BlockSpec((d, LANES), lambda j: (0, 0)),
                  pl.BlockSpec((d, tn), lambda j: (0, j)),
                  pl.BlockSpec((1, tn), lambda j: (0, j))],
        out_specs=pl.BlockSpec((1, tn), lambda j: (0, j)),
        out_shape=jax.ShapeDtypeStruct((1, n), F32),
        compiler_params=_params("arbitrary"),
        name="ada_proj",
    )(cb, w_ada, b_ada.reshape(1, n))


def _rope_table_kernel(pos_ref, inv_ref, cos_ref, sin_ref):
    ang = pos_ref[...].astype(F32) * inv_ref[...]
    lane = lax.broadcasted_iota(jnp.int32, ang.shape, 1)
    cos_ref[...] = jnp.cos(ang)
    s = jnp.sin(ang)
    sin_ref[...] = jnp.where(lane < HEAD_DIM // 2, -s, s)


def _rope_tables(pos):
    n = pos.shape[0]
    half = HEAD_DIM // 2
    inv = ROPE_THETA ** (-jnp.arange(half, dtype=F32) / half)
    inv2 = jnp.concatenate([inv, inv]).reshape(1, HEAD_DIM)
    tm = _tile(n, 1024, SUBLANES)
    return pl.pallas_call(
        _rope_table_kernel,
        grid=(n // tm,),
        in_specs=[pl.BlockSpec((tm, 1), lambda i: (i, 0)),
                  pl.BlockSpec((1, HEAD_DIM), lambda i: (0, 0))],
        out_specs=[pl.BlockSpec((tm, HEAD_DIM), lambda i: (i, 0)),
                   pl.BlockSpec((tm, HEAD_DIM), lambda i: (i, 0))],
        out_shape=[jax.ShapeDtypeStruct((n, HEAD_DIM), F32)] * 2,
        compiler_params=_params("arbitrary"),
        name="rope_tables",
    )(pos.reshape(n, 1), inv2)


def _rms_rope(v, g, cos, sin):
    y = v * lax.rsqrt(jnp.mean(v * v, axis=-1, keepdims=True) + EPS) * g
    return y * cos + pltpu.roll(y, HEAD_DIM // 2, axis=1) * sin


def _prenorm_kernel(x_ref, g_ref, shift_ref, scale_ref, o_ref):
    x = x_ref[...]
    y = x * lax.rsqrt(jnp.mean(x * x, axis=-1, keepdims=True) + EPS) * g_ref[...]
    o_ref[...] = (y * (1.0 + scale_ref[...]) + shift_ref[...]).astype(o_ref.dtype)


def _prenorm(x, g, shift, scale):
    s, d = x.shape
    tm = _tile(s, 256, SUBLANES)
    vec = pl.BlockSpec((1, d), lambda i: (0, 0))
    return pl.pallas_call(
        _prenorm_kernel,
        grid=(s // tm,),
        in_specs=[pl.BlockSpec((tm, d), lambda i: (i, 0)), vec, vec, vec],
        out_specs=pl.BlockSpec((tm, d), lambda i: (i, 0)),
        out_shape=jax.ShapeDtypeStruct((s, d), BF16),
        compiler_params=_params("arbitrary"),
        name="prenorm",
    )(x, g.reshape(1, d), shift.reshape(1, d), scale.reshape(1, d))


def _wblock(w_ref):
    w = w_ref[...]
    return w if w.dtype == BF16 else w.astype(BF16)


def _wspec(w, tn, layer, col_block):
    return pl.BlockSpec((pl.Squeezed(), w.shape[1], tn), lambda i, j: (layer, 0, col_block(j)))


def _mm_glu_kernel(a_ref, w1_ref, w2_ref, b1_ref, b2_ref, o_ref):
    a = a_ref[...]
    lin = _dot(a, _wblock(w1_ref)) + b1_ref[...]
    gate = _dot(a, _wblock(w2_ref)) + b2_ref[...]
    o_ref[...] = (lin * _sigmoid(gate)).astype(o_ref.dtype)


def _mm_glu(a, w, layer, b):
    m, k = a.shape
    n = w.shape[2] // 2
    tm = _tile(m, 1024, SUBLANES)
    tn = _tile(n, 256, LANES)
    nj = n // tn
    b2d = b.reshape(1, 2 * n)
    return pl.pallas_call(
        _mm_glu_kernel,
        grid=(m // tm, nj),
        in_specs=[pl.BlockSpec((tm, k), lambda i, j: (i, 0)),
                  _wspec(w, tn, layer, lambda j: j),
                  _wspec(w, tn, layer, lambda j: j + nj),
                  pl.BlockSpec((1, tn), lambda i, j: (0, j)),
                  pl.BlockSpec((1, tn), lambda i, j: (0, j + nj))],
        out_specs=pl.BlockSpec((tm, tn), lambda i, j: (i, j)),
        out_shape=jax.ShapeDtypeStruct((m, n), F32),
        compiler_params=_params("arbitrary", "arbitrary"),
        name="mm_glu",
    )(a, w, w, b2d, b2d)


def _mm_swiglu_kernel(a_ref, wg_ref, wu_ref, o_ref):
    a = a_ref[...]
    o_ref[...] = (_silu(_dot(a, _wblock(wg_ref))) * _dot(a, _wblock(wu_ref))).astype(o_ref.dtype)


def _mm_swiglu(a, w, layer):
    m, k = a.shape
    n = w.shape[2] // 2
    tm = _tile(m, 1024, SUBLANES)
    tn = _tile(n, 256, LANES)
    nj = n // tn
    return pl.pallas_call(
        _mm_swiglu_kernel,
        grid=(m // tm, nj),
        in_specs=[pl.BlockSpec((tm, k), lambda i, j: (i, 0)),
                  _wspec(w, tn, layer, lambda j: j),
                  _wspec(w, tn, layer, lambda j: j + nj)],
        out_specs=pl.BlockSpec((tm, tn), lambda i, j: (i, j)),
        out_shape=jax.ShapeDtypeStruct((m, n), BF16),
        compiler_params=_params("arbitrary", "arbitrary"),
        name="mm_swiglu",
    )(a, w, w)


def _mm_resid_kernel(a_ref, w_ref, bias_ref, gate_ref, x_ref, o_ref):
    y = _dot(a_ref[...], _wblock(w_ref)) + bias_ref[...]
    o_ref[...] = x_ref[...] + gate_ref[...] * y


def _mm_resid(a, w, layer, bias, gate, x):
    m, k = a.shape
    n = w.shape[2]
    big_k = k > 8192
    tm = _tile(m, 512 if big_k else 1024, SUBLANES)
    tn = _tile(n, 256 if big_k else 512, LANES)
    vec = pl.BlockSpec((1, tn), lambda i, j: (0, j))
    return pl.pallas_call(
        _mm_resid_kernel,
        grid=(m // tm, n // tn),
        in_specs=[pl.BlockSpec((tm, k), lambda i, j: (i, 0)),
                  _wspec(w, tn, layer, lambda j: j),
                  vec, vec,
                  pl.BlockSpec((tm, tn), lambda i, j: (i, j))],
        out_specs=pl.BlockSpec((tm, tn), lambda i, j: (i, j)),
        out_shape=jax.ShapeDtypeStruct((m, n), F32),
        compiler_params=_params("arbitrary", "arbitrary"),
        name="mm_resid",
    )(a, w, bias.reshape(1, n), gate.reshape(1, n), x)


def _mm_sigmoid_kernel(a_ref, w_ref, b_ref, o_ref):
    o_ref[...] = _sigmoid(_dot(a_ref[...], _wblock(w_ref)) + b_ref[...])


def _mm_sigmoid(a, w, layer, b):
    m, k = a.shape
    n = w.shape[2]
    tm = _tile(m, 1024, SUBLANES)
    return pl.pallas_call(
        _mm_sigmoid_kernel,
        grid=(m // tm, 1),
        in_specs=[pl.BlockSpec((tm, k), lambda i, j: (i, 0)),
                  _wspec(w, n, layer, lambda j: 0),
                  pl.BlockSpec((1, n), lambda i, j: (0, 0))],
        out_specs=pl.BlockSpec((tm, n), lambda i, j: (i, 0)),
        out_shape=jax.ShapeDtypeStruct((m, n), F32),
        compiler_params=_params("arbitrary", "arbitrary"),
        name="mm_sigmoid",
    )(a, w, b.reshape(1, n))


def _mm_heads_kernel(a_ref, w_ref, g_ref, cos_ref, sin_ref, o_ref, *, rope_every, out_scale):
    j = pl.program_id(1)
    nh = o_ref.shape[0]
    y = _dot(a_ref[...], _wblock(w_ref))

    def plain():
        for h in range(nh):
            o_ref[h] = y[:, h * HEAD_DIM:(h + 1) * HEAD_DIM].astype(o_ref.dtype)

    def roped():
        g, cos, sin = g_ref[0], cos_ref[...], sin_ref[...]
        for h in range(nh):
            v = _rms_rope(y[:, h * HEAD_DIM:(h + 1) * HEAD_DIM], g, cos, sin)
            o_ref[h] = (v * out_scale).astype(o_ref.dtype)

    if rope_every == 0:
        plain()
    elif rope_every == 1:
        roped()
    else:
        pl.when(j % rope_every == 0)(roped)
        pl.when(j % rope_every != 0)(plain)


def _mm_heads(a, w, layer, col0, n_cols, gains, gain_of_block, cos, sin, *, rope_every, out_scale, out_dtype):
    m, k = a.shape
    tn = 4 * HEAD_DIM
    nh = tn // HEAD_DIM
    tm = _tile(m, 1024, SUBLANES)
    j0 = col0 // tn
    ng = gains.shape[0]
    return pl.pallas_call(
        functools.partial(_mm_heads_kernel, rope_every=rope_every, out_scale=out_scale),
        grid=(m // tm, n_cols // tn),
        in_specs=[pl.BlockSpec((tm, k), lambda i, j: (i, 0)),
                  _wspec(w, tn, layer, lambda j: j + j0),
                  pl.BlockSpec((1, 1, HEAD_DIM), lambda i, j: (gain_of_block(j), 0, 0)),
                  pl.BlockSpec((tm, HEAD_DIM), lambda i, j: (i, 0)),
                  pl.BlockSpec((tm, HEAD_DIM), lambda i, j: (i, 0))],
        out_specs=pl.BlockSpec((nh, tm, HEAD_DIM), lambda i, j: (j, i, 0)),
        out_shape=jax.ShapeDtypeStruct((n_cols // HEAD_DIM, m, HEAD_DIM), out_dtype),
        compiler_params=_params("arbitrary", "arbitrary"),
        name="mm_heads",
    )(a, w, gains.reshape(ng, 1, HEAD_DIM), cos, sin)


CONV_ROWS = 64
CONV_HALO = 32


def _conv_kernel(halo_ref, u_ref, wdw_ref, bdw_ref, g_ref, b_ref, o_ref, buf_ref, acc_ref):
    i = pl.program_id(0)
    tm, d = u_ref.shape
    cw = wdw_ref.shape[0]
    buf_ref[pl.ds(CONV_HALO, tm), :] = u_ref[...]
    buf_ref[pl.ds(0, CONV_HALO), :] = jnp.where(i > 0, halo_ref[...], 0.0)
    off = CONV_HALO - (cw - 1)

    def col_body(c, carry):
        c0 = pl.multiple_of(c * LANES, LANES)
        acc = jnp.zeros((tm, LANES), F32)
        for b in range(SUBLANES):
            taps = [(a, SUBLANES * a + b - off) for a in range((off + cw - 1) // SUBLANES + 1)
                    if 0 <= SUBLANES * a + b - off < cw]
            n_rows = SUBLANES * taps[-1][0] + tm
            shifted = buf_ref[pl.ds(b, n_rows), pl.ds(c0, LANES)]
            for a, w in taps:
                acc = acc + shifted[SUBLANES * a:SUBLANES * a + tm, :] * wdw_ref[pl.ds(w, 1), pl.ds(c0, LANES)]
        acc_ref[:, pl.ds(c0, LANES)] = acc + bdw_ref[:, pl.ds(c0, LANES)]
        return carry

    lax.fori_loop(0, d // LANES, col_body, 0)
    y = acc_ref[...]
    mu = jnp.mean(y, axis=-1, keepdims=True)
    yc = y - mu
    var = jnp.mean(yc * yc, axis=-1, keepdims=True)
    z = yc * lax.rsqrt(var + EPS) * g_ref[...] + b_ref[...]
    o_ref[...] = _silu(z).astype(o_ref.dtype)


def _conv_ln_silu(u, w_dw, b_dw, ln_g, ln_b):
    s, d = u.shape
    cw = w_dw.shape[0]
    assert cw - 1 <= CONV_HALO
    tm = _tile(s, CONV_ROWS, CONV_HALO)
    ratio = tm // CONV_HALO
    vec = pl.BlockSpec((1, d), lambda i: (0, 0))
    return pl.pallas_call(
        _conv_kernel,
        grid=(s // tm,),
        in_specs=[pl.BlockSpec((CONV_HALO, d), lambda i: (jnp.maximum(i * ratio - 1, 0), 0)),
                  pl.BlockSpec((tm, d), lambda i: (i, 0)),
                  pl.BlockSpec((cw, d), lambda i: (0, 0)),
                  vec, vec, vec],
        out_specs=pl.BlockSpec((tm, d), lambda i: (i, 0)),
        out_shape=jax.ShapeDtypeStruct((s, d), BF16),
        scratch_shapes=[pltpu.VMEM((CONV_HALO + tm, d), F32), pltpu.VMEM((tm, d), F32)],
        compiler_params=_params("arbitrary"),
        name="conv_ln_silu",
    )(u, u, w_dw, b_dw.reshape(1, d), ln_g.reshape(1, d), ln_b.reshape(1, d))


def _compress_kernel(c_ref, p_ref, w1_ref, b1_ref, w2_ref, b2_ref, g_ref, cos_ref, sin_ref, o_ref):
    j = pl.program_id(0)
    nch, half = c_ref.shape
    c = c_ref[...]
    a_lo = (c + p_ref[0:1, :]).astype(BF16)
    a_hi = (c + p_ref[1:2, :]).astype(BF16)
    h_lo = _dot(a_lo, w1_ref[pl.ds(0, half), :])
    h_hi = _dot(a_hi, w1_ref[pl.ds(half, half), :])
    hid = h_lo + pltpu.roll(h_hi, nch - 1, axis=0) + b1_ref[...]
    y = _dot(_silu(hid).astype(BF16), w2_ref[...]) + b2_ref[...]

    @pl.when(j == 0)
    def _():
        o_ref[...] = _rms_rope(y, g_ref[...], cos_ref[...], sin_ref[...]).astype(o_ref.dtype)

    @pl.when(j != 0)
    def _():
        o_ref[...] = y.astype(o_ref.dtype)


def _compress(raw, cmp_pos, w1, b1, w2, b2, g, cos_c, sin_c):
    _, ng, s, dh = raw.shape
    nch = s // CMP_STRIDE
    half = CMP_STRIDE * dh
    hid = w1.shape[-1]
    chunks = raw.reshape(2, ng, nch, half)
    pos2 = cmp_pos.reshape(2, CMP_LEN // CMP_STRIDE, half)
    sq = pl.Squeezed()
    return pl.pallas_call(
        _compress_kernel,
        grid=(2, ng),
        in_specs=[pl.BlockSpec((sq, sq, nch, half), lambda j, gi: (j, gi, 0, 0)),
                  pl.BlockSpec((sq, CMP_LEN // CMP_STRIDE, half), lambda j, gi: (j, 0, 0)),
                  pl.BlockSpec((sq, 2 * half, hid), lambda j, gi: (j, 0, 0)),
                  pl.BlockSpec((sq, 1, hid), lambda j, gi: (j, 0, 0)),
                  pl.BlockSpec((sq, hid, dh), lambda j, gi: (j, 0, 0)),
                  pl.BlockSpec((sq, 1, dh), lambda j, gi: (j, 0, 0)),
                  pl.BlockSpec((1, dh), lambda j, gi: (0, 0)),
                  pl.BlockSpec((nch, dh), lambda j, gi: (0, 0)),
                  pl.BlockSpec((nch, dh), lambda j, gi: (0, 0))],
        out_specs=pl.BlockSpec((sq, sq, nch, dh), lambda j, gi: (j, gi, 0, 0)),
        out_shape=jax.ShapeDtypeStruct((2, ng, nch, dh), BF16),
        compiler_params=_params("arbitrary", "arbitrary"),
        name="compress",
    )(chunks, pos2, w1, b1.reshape(2, 1, hid), w2, b2.reshape(2, 1, dh), g.reshape(1, dh), cos_c, sin_c)


NSA_TQ = 128
NSA_TK = 512


def _nsa_kernel(q_ref, kc_ref, vc_ref, ksa_ref, vsa_ref, kw_ref, vw_ref, gate_ref, mt_ref, o_ref,
                imp_ref, qa_ref, sa_ref, sb_ref, pc_ref, pw_ref, ps_ref, m_ref, alpha_ref, acc_ref, out_ref):
    r, tq, dh = q_ref.shape
    rows = r * tq
    nc = kc_ref.shape[0]
    ns = mt_ref.shape[0]
    s_len = kw_ref.shape[0]
    n_aug = ksa_ref.shape[1] - dh
    tk = min(NSA_TK, s_len)
    wk = min(WINDOW + tq, s_len)
    qt = pl.program_id(1)
    t0 = qt * tq
    q2 = q_ref[...].reshape(rows, dh)
    t_q = t0 + lax.broadcasted_iota(jnp.int32, (tq, 1), 0)
    gates = gate_ref[...]
    heads = [slice(h * tq, (h + 1) * tq) for h in range(r)]

    k_start = pl.multiple_of(jnp.maximum(jnp.minimum(t0 + tq, s_len) - wk, 0), SUBLANES * 2)
    sa_ref[:, :nc] = _dot_nt(q2, kc_ref[...])
    sb_ref[:, :wk] = _dot_nt(q2, kw_ref[pl.ds(k_start, wk), :])

    cmp_end = lax.broadcasted_iota(jnp.int32, (1, nc), 1) * CMP_STRIDE + (CMP_LEN - 1)
    bias_c = jnp.where(cmp_end <= t_q, 0.0, NEG_INF)
    has_block = t_q >= CMP_LEN - 1
    p_sum = jnp.zeros((tq, nc), F32)
    for rs in heads:
        s = sa_ref[rs, :nc] + bias_c
        e = jnp.exp2(s - jnp.max(s, axis=-1, keepdims=True))
        l = jnp.sum(e, axis=-1, keepdims=True)
        p = e * jnp.where(has_block, 1.0 / l, 0.0)
        p_sum = p_sum + p
        pc_ref[rs, :] = p.astype(BF16)

    back = t_q - (k_start + lax.broadcasted_iota(jnp.int32, (1, wk), 1))
    bias_w = jnp.where((back >= 0) & (back < WINDOW), 0.0, NEG_INF)
    for rs in heads:
        s = sb_ref[rs, :wk] + bias_w
        e = jnp.exp2(s - jnp.max(s, axis=-1, keepdims=True))
        pw_ref[rs, :] = (e * (1.0 / jnp.sum(e, axis=-1, keepdims=True))).astype(BF16)

    o_c = _dot(pc_ref[...], vc_ref[...])
    o_w = _dot(pw_ref[...], vw_ref[pl.ds(k_start, wk), :])
    for h, rs in enumerate(heads):
        out_ref[rs, :] = gates[:, h:h + 1] * o_c[rs, :] + gates[:, 2 * r + h:2 * r + h + 1] * o_w[rs, :]

    p_hi = p_sum.astype(BF16)
    rem = p_sum - p_hi.astype(F32)
    p_mid = rem.astype(BF16)
    p_lo = (rem - p_mid.astype(F32)).astype(BF16)
    mt = mt_ref[...]
    imp = _dot_nt(mt, p_hi) + _dot_nt(mt, p_mid) + _dot_nt(mt, p_lo)

    t_row = t0 + lax.broadcasted_iota(jnp.int32, (1, tq), 1)
    blk = lax.broadcasted_iota(jnp.int32, (ns, 1), 0)
    dist = t_row // SLC_BLOCK - blk
    forced = (blk < N_INIT_BLOCKS) | ((dist >= 0) & (dist < N_LOCAL_BLOCKS))
    imp = jnp.where(forced, jnp.inf, jnp.where(dist >= 0, imp, -jnp.inf))
    imp_ref[...] = imp

    n_rivals = jnp.minimum((t0 + tq - 1) // SLC_BLOCK + 1, ns)

    def rank_body(i, cnt):
        row = imp_ref[pl.ds(i, 1), :]
        beats = (row > imp) | ((row == imp) & (i < blk))
        return cnt + jnp.where(beats, 1.0, 0.0)

    cnt = lax.fori_loop(0, n_rivals, rank_body, jnp.zeros((ns, tq), F32))
    chosen = (cnt < float(min(SLC_TOPK, ns))) & (dist >= 0)
    not_sel = jnp.where(chosen, 0.0, 1.0).T.astype(BF16)
    if n_aug > ns:
        not_sel = jnp.concatenate([not_sel, jnp.zeros((tq, n_aug - ns), BF16)], axis=1)
    qa_ref[:, :dh] = q2
    for rs in heads:
        qa_ref[rs, dh:] = not_sel

    m_ref[...] = jnp.full(m_ref.shape, NEG_INF, F32)
    acc_ref[...] = jnp.zeros(acc_ref.shape, F32)
    n_tiles = (t0 + tq + tk - 1) // tk
    last_tile = s_len // tk - 1

    def scores(kt):
        k0 = pl.multiple_of(jnp.minimum(kt, last_tile) * tk, tk)
        return _dot_nt(qa_ref[...], ksa_ref[pl.ds(k0, tk), :])

    def absorb(s_ref, kt):
        kpos = kt * tk + lax.broadcasted_iota(jnp.int32, (1, tk), 1)
        bias = jnp.where(kpos <= t_q, 0.0, NEG_INF)
        for rs in heads:
            s = s_ref[rs, :tk] + bias
            m_old = m_ref[rs, :]
            m_new = jnp.maximum(m_old, jnp.max(s, axis=-1, keepdims=True))
            ps_ref[rs, :] = jnp.exp2(s - _lane_tile(m_new, tk)).astype(BF16)
            alpha_ref[rs, :] = jnp.exp2(m_old - m_new)
            m_ref[rs, :] = m_new
        k0 = pl.multiple_of(jnp.minimum(kt, last_tile) * tk, tk)
        pv = _dot(ps_ref[...], vsa_ref[pl.ds(k0, tk), :])
        acc_ref[...] = acc_ref[...] * _lane_tile(alpha_ref[...], acc_ref.shape[1]) + pv

    sa_ref[:, :tk] = scores(0)

    def pair_body(j, carry):
        kt = 2 * j
        sb_ref[:, :tk] = scores(kt + 1)
        absorb(sa_ref, kt)
        sa_ref[:, :tk] = scores(kt + 2)
        absorb(sb_ref, kt + 1)
        return carry

    lax.fori_loop(0, (n_tiles + 1) // 2, pair_body, 0)

    for h, rs in enumerate(heads):
        o_s = acc_ref[rs, :dh] / acc_ref[rs, dh:2 * dh]
        o_ref[:, h * dh:(h + 1) * dh] = (out_ref[rs, :] + gates[:, r + h:r + h + 1] * o_s).astype(o_ref.dtype)


def _cmp_to_slc_t(n_chunks, n_slc):
    n_cmp = n_chunks - 1
    r, cl = SLC_BLOCK // CMP_STRIDE, CMP_LEN // CMP_STRIDE
    offs = (np.arange(r)[:, None] - np.arange(cl)[None, :]).reshape(-1)
    tgt = r * np.arange(n_slc)[None, :, None] + offs[None, None, :]
    m = (np.arange(n_cmp)[:, None, None] == tgt).sum(-1).astype(np.float32)
    out = np.zeros((n_slc, n_chunks), np.float32)
    out[:, :n_cmp] = m.T
    return out


def _augment_selected_kv(k_slc, v_slc):
    ng, s, dh = k_slc.shape
    ns = s // SLC_BLOCK
    n_aug = _round_up(ns, LANES)
    own_block = (np.arange(s)[:, None] // SLC_BLOCK) == np.arange(n_aug)[None, :]
    marker = jnp.asarray(np.where(own_block, NEG_INF, 0.0), BF16)
    k_aug = jnp.concatenate([k_slc, jnp.broadcast_to(marker, (ng, s, n_aug))], axis=-1)
    v_aug = jnp.concatenate([v_slc, jnp.ones((ng, s, dh), BF16)], axis=-1)
    return k_aug, v_aug


def _nsa_attention(q, k_cmp, v_cmp, k_aug, v_aug, k_win, v_win, gates):
    n_heads, s, dh = q.shape
    ng = k_win.shape[0]
    r = n_heads // ng
    nc = k_cmp.shape[1]
    ns = s // SLC_BLOCK
    tq = _tile(s, NSA_TQ, SLC_BLOCK)
    rows = r * tq
    tk = min(NSA_TK, s)
    wk = min(WINDOW + tq, s)
    width = max(nc, tk, wk)
    mt = jnp.asarray(_cmp_to_slc_t(nc, ns), BF16)
    sq = pl.Squeezed()
    per_group = lambda arr: pl.BlockSpec((sq,) + arr.shape[1:], lambda gi, qi: (gi, 0, 0))
    return pl.pallas_call(
        _nsa_kernel,
        grid=(ng, s // tq),
        in_specs=[pl.BlockSpec((r, tq, dh), lambda gi, qi: (gi, qi, 0)),
                  per_group(k_cmp), per_group(v_cmp), per_group(k_aug), per_group(v_aug),
                  per_group(k_win), per_group(v_win),
                  pl.BlockSpec((sq, tq, 3 * r), lambda gi, qi: (gi, qi, 0)),
                  pl.BlockSpec((ns, nc), lambda gi, qi: (0, 0))],
        out_specs=pl.BlockSpec((tq, r * dh), lambda gi, qi: (qi, gi)),
        out_shape=jax.ShapeDtypeStruct((s, n_heads * dh), BF16),
        scratch_shapes=[pltpu.VMEM((ns, tq), F32),
                        pltpu.VMEM((rows, k_aug.shape[2]), BF16),
                        pltpu.VMEM((rows, width), F32),
                        pltpu.VMEM((rows, width), F32),
                        pltpu.VMEM((rows, nc), BF16),
                        pltpu.VMEM((rows, wk), BF16),
                        pltpu.VMEM((rows, tk), BF16),
                        pltpu.VMEM((rows, LANES), F32),
                        pltpu.VMEM((rows, LANES), F32),
                        pltpu.VMEM((rows, v_aug.shape[2]), F32),
                        pltpu.VMEM((rows, dh), F32)],
        compiler_params=_params("arbitrary", "arbitrary"),
        name="nsa_attention",
    )(q, k_cmp, v_cmp, k_aug, v_aug, k_win, v_win, gates, mt)


def kernel(x, c, positions, w_ada, b_ada, ada_emb, kv_ada_emb, norm_mix, norm_ffn, norm_kv, conv_w_pw1, conv_b_pw1, conv_w_dw, conv_b_dw, conv_ln_g, conv_ln_b, conv_w_pw2, conv_b_pw2, nsa_w_q, nsa_q_norm, nsa_w_gate, nsa_b_gate, nsa_w_o, kv_w, kv_k_norm, cmp_pos, cmp_w1, cmp_b1, cmp_w2, cmp_b2, ffn_w_gu, ffn_w_down):
    batch, s, d = x.shape
    assert batch == 1, "the kernels are written for a single sequence"
    depth = ada_emb.shape[0]
    n_conv = conv_w_pw1.shape[0]
    ng = N_KV_GROUPS
    n_heads = d // HEAD_DIM
    r = n_heads // ng
    kvw = ng * HEAD_DIM
    zeros_d = jnp.zeros((d,), F32)

    xs = x[0]
    mod = _ada_proj(c, w_ada, b_ada).reshape(6, d)
    pos = positions[0]
    cos, sin = _rope_tables(pos)
    kv = None
    w_down = ffn_w_down.astype(BF16)
    for layer in range(depth):
        m = mod + ada_emb[layer]
        h = _prenorm(xs, norm_mix[layer], m[0], m[1])
        if layer < n_conv:
            u = _mm_glu(h, conv_w_pw1, layer, conv_b_pw1[layer])
            v = _conv_ln_silu(u, conv_w_dw[layer], conv_b_dw[layer], conv_ln_g[layer], conv_ln_b[layer])
            xs = _mm_resid(v, conv_w_pw2, layer, conv_b_pw2[layer], m[2], xs)
        else:
            if kv is None:
                h_kv = _prenorm(xs, norm_kv, mod[0] + kv_ada_emb[0], mod[1] + kv_ada_emb[1])
                kv_w3 = kv_w[None]
                raw = _mm_heads(h_kv, kv_w3, 0, 0, 2 * kvw, kv_k_norm, lambda j: 0, cos, sin,
                                rope_every=0, out_scale=1.0, out_dtype=F32)
                rest = _mm_heads(h_kv, kv_w3, 0, 2 * kvw, 4 * kvw, kv_k_norm, lambda j: 1 + j // 2, cos, sin,
                                 rope_every=2, out_scale=1.0, out_dtype=BF16)
                rest = rest.reshape(4, ng, s, HEAD_DIM)
                n_chunks = s // CMP_STRIDE
                pos_c = jnp.pad(pos[CMP_LEN - 1::CMP_STRIDE], (0, 1))[:n_chunks]
                cos_c, sin_c = _rope_tables(pos_c)
                cmp = _compress(raw.reshape(2, ng, s, HEAD_DIM), cmp_pos, cmp_w1.astype(BF16), cmp_b1,
                                cmp_w2.astype(BF16), cmp_b2, kv_k_norm[0], cos_c, sin_c)
                k_aug, v_aug = _augment_selected_kv(rest[0], rest[1])
                kv = (cmp[0], cmp[1], k_aug, v_aug, rest[2], rest[3])
            i = layer - n_conv
            q = _mm_heads(h, nsa_w_q, i, 0, n_heads * HEAD_DIM, nsa_q_norm[i:i + 1],
                          lambda j: 0, cos, sin, rope_every=1, out_scale=HEAD_DIM ** -0.5 * LOG2_E,
                          out_dtype=BF16)
            gates = _mm_sigmoid(h, nsa_w_gate, i, nsa_b_gate[i])
            gates = gates.reshape(s, 3, ng, r).transpose(2, 0, 1, 3).reshape(ng, s, 3 * r)
            o = _nsa_attention(q, *kv, gates)
            xs = _mm_resid(o, nsa_w_o, i, zeros_d, m[2], xs)
        h = _prenorm(xs, norm_ffn[layer], m[3], m[4])
        a = _mm_swiglu(h, ffn_w_gu, layer)
        xs = _mm_resid(a, w_down, layer, zeros_d, m[5], xs)
    return xs[None]
```

```python
import functools
import math

import numpy as np
import jax
import jax.numpy as jnp
from jax import lax
from jax.experimental import pallas as pl
from jax.experimental.pallas import tpu as pltpu

HEAD_DIM = 128
N_KV_GROUPS = 4
CMP_LEN = 32
CMP_STRIDE = 16
SLC_BLOCK = 64
SLC_TOPK = 16
N_INIT_BLOCKS = 1
N_LOCAL_BLOCKS = 2
WINDOW = 512
ROPE_THETA = 10000.0
EPS = 1e-6
NEG_INF = -1e30
LOG2_E = math.log2(math.e)

LANES = 128
SUBLANES = 8
VMEM_LIMIT_BYTES = 56 * 1024 * 1024

BF16 = jnp.bfloat16
F32 = jnp.float32


def _params(*sem):
    return pltpu.CompilerParams(dimension_semantics=sem, vmem_limit_bytes=VMEM_LIMIT_BYTES)


def _tile(n, target, quantum):
    if n <= target:
        return n
    t = (target // quantum) * quantum
    while t >= quantum:
        if n % t == 0:
            return t
        t -= quantum
    raise ValueError(f"no tile for {n} with quantum {quantum}")


def _round_up(n, q):
    return -(-n // q) * q


def _dot(a, b):
    return jnp.dot(a, b, preferred_element_type=F32)


def _dot_nt(a, b):
    return lax.dot_general(a, b, (((1,), (1,)), ((), ())), preferred_element_type=F32)


def _sigmoid(v):
    return 1.0 / (1.0 + jnp.exp(-v))


def _silu(v):
    return v * _sigmoid(v)


def _lane_tile(v, width):
    reps = width // v.shape[1]
    return v if reps == 1 else jnp.concatenate([v] * reps, axis=1)


def _ada_kernel(cb_ref, w_ref, b_ref, o_ref):
    d, tn = w_ref.shape
    n_chunks = tn // LANES

    def body(k, accs):
        r = pl.multiple_of(k * SUBLANES, SUBLANES)
        cv = _silu(cb_ref[pl.ds(r, SUBLANES), :])
        return tuple(
            accs[j] + cv * w_ref[pl.ds(r, SUBLANES), j * LANES:(j + 1) * LANES]
            for j in range(n_chunks))

    accs = lax.fori_loop(0, d // SUBLANES, body,
                         tuple(jnp.zeros((SUBLANES, LANES), F32) for _ in range(n_chunks)), unroll=4)
    for j in range(n_chunks):
        o_ref[:, j * LANES:(j + 1) * LANES] = (
            jnp.sum(accs[j], axis=0, keepdims=True) + b_ref[:, j * LANES:(j + 1) * LANES])


def _ada_proj(c, w_ada, b_ada):
    d, n = w_ada.shape
    tn = _tile(n, 512, LANES)
    cb = jnp.broadcast_to(c.reshape(d, 1), (d, LANES))
    return pl.pallas_call(
        _ada_kernel,
        grid=(n // tn,),
        in_specs=[pl.BlockSpec((d, LANES), lambda j: (0, 0)),
                  pl.BlockSpec((d, tn), lambda j: (0, j)),
                  pl.BlockSpec((1, tn), lambda j: (0, j))],
        out_specs=pl.BlockSpec((1, tn), lambda j: (0, j)),
        out_shape=jax.ShapeDtypeStruct((1, n), F32),
        compiler_params=_params("arbitrary"),
        name="ada_proj",
    )(cb, w_ada, b_ada.reshape(1, n))


def _rope_table_kernel(pos_ref, inv_ref, cos_ref, sin_ref):
    ang = pos_ref[...].astype(F32) * inv_ref[...]
    lane = lax.broadcasted_iota(jnp.int32, ang.shape, 1)
    cos_ref[...] = jnp.cos(ang)
    s = jnp.sin(ang)
    sin_ref[...] = jnp.where(lane < HEAD_DIM // 2, -s, s)


def _rope_tables(pos):
    n = pos.shape[0]
    half = HEAD_DIM // 2
    inv = ROPE_THETA ** (-jnp.arange(half, dtype=F32) / half)
    inv2 = jnp.concatenate([inv, inv]).reshape(1, HEAD_DIM)
    tm = _tile(n, 1024, SUBLANES)
    return pl.pallas_call(
        _rope_table_kernel,
        grid=(n // tm,),
        in_specs=[pl.BlockSpec((tm, 1), lambda i: (i, 0)),
                  pl.BlockSpec((1, HEAD_DIM), lambda i: (0, 0))],
        out_specs=[pl.BlockSpec((tm, HEAD_DIM), lambda i: (i, 0)),
                   pl.BlockSpec((tm, HEAD_DIM), lambda i: (i, 0))],
        out_shape=[jax.ShapeDtypeStruct((n, HEAD_DIM), F32)] * 2,
        compiler_params=_params("arbitrary"),
        name="rope_tables",
    )(pos.reshape(n, 1), inv2)


def _rms_rope(v, g, cos, sin):
    y = v * lax.rsqrt(jnp.mean(v * v, axis=-1, keepdims=True) + EPS) * g
    return y * cos + pltpu.roll(y, HEAD_DIM // 2, axis=1) * sin


def _prenorm_kernel(x_ref, g_ref, shift_ref, scale_ref, o_ref):
    x = x_ref[...]
    y = x * lax.rsqrt(jnp.mean(x * x, axis=-1, keepdims=True) + EPS) * g_ref[...]
    o_ref[...] = (y * (1.0 + scale_ref[...]) + shift_ref[...]).astype(o_ref.dtype)


def _prenorm(x, g, shift, scale):
    s, d = x.shape
    tm = _tile(s, 256, SUBLANES)
    vec = pl.BlockSpec((1, d), lambda i: (0, 0))
    return pl.pallas_call(
        _prenorm_kernel,
        grid=(s // tm,),
        in_specs=[pl.BlockSpec((tm, d), lambda i: (i, 0)), vec, vec, vec],
        out_specs=pl.BlockSpec((tm, d), lambda i: (i, 0)),
        out_shape=jax.ShapeDtypeStruct((s, d), BF16),
        compiler_params=_params("arbitrary"),
        name="prenorm",
    )(x, g.reshape(1, d), shift.reshape(1, d), scale.reshape(1, d))


def _wblock(w_ref):
    w = w_ref[...]
    return w if w.dtype == BF16 else w.astype(BF16)


def _wspec(w, tn, layer, col_block):
    return pl.BlockSpec((pl.Squeezed(), w.shape[1], tn), lambda i, j: (layer, 0, col_block(j)))


def _mm_glu_kernel(a_ref, w1_ref, w2_ref, b1_ref, b2_ref, o_ref):
    a = a_ref[...]
    lin = _dot(a, _wblock(w1_ref)) + b1_ref[...]
    gate = _dot(a, _wblock(w2_ref)) + b2_ref[...]
    o_ref[...] = (lin * _sigmoid(gate)).astype(o_ref.dtype)


def _mm_glu(a, w, layer, b):
    m, k = a.shape
    n = w.shape[2] // 2
    tm = _tile(m, 1024, SUBLANES)
    tn = _tile(n, 256, LANES)
    nj = n // tn
    b2d = b.reshape(1, 2 * n)
    return pl.pallas_call(
        _mm_glu_kernel,
        grid=(m // tm, nj),
        in_specs=[pl.BlockSpec((tm, k), lambda i, j: (i, 0)),
                  _wspec(w, tn, layer, lambda j: j),
                  _wspec(w, tn, layer, lambda j: j + nj),
                  pl.BlockSpec((1, tn), lambda i, j: (0, j)),
                  pl.BlockSpec((1, tn), lambda i, j: (0, j + nj))],
        out_specs=pl.BlockSpec((tm, tn), lambda i, j: (i, j)),
        out_shape=jax.ShapeDtypeStruct((m, n), F32),
        compiler_params=_params("arbitrary", "arbitrary"),
        name="mm_glu",
    )(a, w, w, b2d, b2d)


def _mm_swiglu_kernel(a_ref, wg_ref, wu_ref, o_ref):
    a = a_ref[...]
    o_ref[...] = (_silu(_dot(a, _wblock(wg_ref))) * _dot(a, _wblock(wu_ref))).astype(o_ref.dtype)


def _mm_swiglu(a, w, layer):
    m, k = a.shape
    n = w.shape[2] // 2
    tm = _tile(m, 1024, SUBLANES)
    tn = _tile(n, 256, LANES)
    nj = n // tn
    return pl.pallas_call(
        _mm_swiglu_kernel,
        grid=(m // tm, nj),
        in_specs=[pl.BlockSpec((tm, k), lambda i, j: (i, 0)),
                  _wspec(w, tn, layer, lambda j: j),
                  _wspec(w, tn, layer, lambda j: j + nj)],
        out_specs=pl.BlockSpec((tm, tn), lambda i, j: (i, j)),
        out_shape=jax.ShapeDtypeStruct((m, n), BF16),
        compiler_params=_params("arbitrary", "arbitrary"),
        name="mm_swiglu",
    )(a, w, w)


def _mm_resid_kernel(a_ref, w_ref, bias_ref, gate_ref, x_ref, o_ref):
    y = _dot(a_ref[...], _wblock(w_ref)) + bias_ref[...]
    o_ref[...] = x_ref[...] + gate_ref[...] * y


def _mm_resid(a, w, layer, bias, gate, x):
    m, k = a.shape
    n = w.shape[2]
    big_k = k > 8192
    tm = _tile(m, 512 if big_k else 1024, SUBLANES)
    tn = _tile(n, 256 if big_k else 512, LANES)
    vec = pl.BlockSpec((1, tn), lambda i, j: (0, j))
    return pl.pallas_call(
        _mm_resid_kernel,
        grid=(m // tm, n // tn),
        in_specs=[pl.BlockSpec((tm, k), lambda i, j: (i, 0)),
                  _wspec(w, tn, layer, lambda j: j),
                  vec, vec,
                  pl.BlockSpec((tm, tn), lambda i, j: (i, j))],
        out_specs=pl.BlockSpec((tm, tn), lambda i, j: (i, j)),
        out_shape=jax.ShapeDtypeStruct((m, n), F32),
        compiler_params=_params("arbitrary", "arbitrary"),
        name="mm_resid",
    )(a, w, bias.reshape(1, n), gate.reshape(1, n), x)


def _mm_sigmoid_kernel(a_ref, w_ref, b_ref, o_ref):
    o_ref[...] = _sigmoid(_dot(a_ref[...], _wblock(w_ref)) + b_ref[...])


def _mm_sigmoid(a, w, layer, b):
    m, k = a.shape
    n = w.shape[2]
    tm = _tile(m, 1024, SUBLANES)
    return pl.pallas_call(
        _mm_sigmoid_kernel,
        grid=(m // tm, 1),
        in_specs=[pl.BlockSpec((tm, k), lambda i, j: (i, 0)),
                  _wspec(w, n, layer, lambda j: 0),
                  pl.BlockSpec((1, n), lambda i, j: (0, 0))],
        out_specs=pl.BlockSpec((tm, n), lambda i, j: (i, 0)),
        out_shape=jax.ShapeDtypeStruct((m, n), F32),
        compiler_params=_params("arbitrary", "arbitrary"),
        name="mm_sigmoid",
    )(a, w, b.reshape(1, n))


HEADS_CHUNK = 128


def _mm_heads_kernel(a_ref, w_ref, g_ref, cos_ref, sin_ref, o_ref, y_ref, *, rope_every, out_scale):
    j = pl.program_id(1)
    nh, tm, _ = o_ref.shape
    y = _dot(a_ref[...], _wblock(w_ref))

    def plain():
        for h in range(nh):
            o_ref[h] = y[:, h * HEAD_DIM:(h + 1) * HEAD_DIM].astype(o_ref.dtype)

    def roped():
        y_ref[...] = y
        g = g_ref[0]
        chunk = min(HEADS_CHUNK, tm)

        def body(c, carry):
            rs = pl.ds(pl.multiple_of(c * chunk, chunk), chunk)
            cos, sin = cos_ref[rs, :], sin_ref[rs, :]
            for h in range(nh):
                v = _rms_rope(y_ref[rs, h * HEAD_DIM:(h + 1) * HEAD_DIM], g, cos, sin)
                o_ref[h, rs, :] = (v * out_scale).astype(o_ref.dtype)
            return carry

        lax.fori_loop(0, tm // chunk, body, 0, unroll=2)

    if rope_every == 0:
        plain()
    elif rope_every == 1:
        roped()
    else:
        pl.when(j % rope_every == 0)(roped)
        pl.when(j % rope_every != 0)(plain)


def _mm_heads(a, w, layer, col0, n_cols, gains, gain_of_block, cos, sin, *, rope_every, out_scale, out_dtype):
    m, k = a.shape
    tn = 4 * HEAD_DIM
    nh = tn // HEAD_DIM
    tm = _tile(m, 1024, SUBLANES)
    j0 = col0 // tn
    ng = gains.shape[0]
    return pl.pallas_call(
        functools.partial(_mm_heads_kernel, rope_every=rope_every, out_scale=out_scale),
        grid=(m // tm, n_cols // tn),
        in_specs=[pl.BlockSpec((tm, k), lambda i, j: (i, 0)),
                  _wspec(w, tn, layer, lambda j: j + j0),
                  pl.BlockSpec((1, 1, HEAD_DIM), lambda i, j: (gain_of_block(j), 0, 0)),
                  pl.BlockSpec((tm, HEAD_DIM), lambda i, j: (i, 0)),
                  pl.BlockSpec((tm, HEAD_DIM), lambda i, j: (i, 0))],
        out_specs=pl.BlockSpec((nh, tm, HEAD_DIM), lambda i, j: (j, i, 0)),
        out_shape=jax.ShapeDtypeStruct((n_cols // HEAD_DIM, m, HEAD_DIM), out_dtype),
        scratch_shapes=[pltpu.VMEM((tm, tn), F32)],
        compiler_params=_params("arbitrary", "arbitrary"),
        name="mm_heads",
    )(a, w, gains.reshape(ng, 1, HEAD_DIM), cos, sin)


CONV_ROWS = 64
CONV_HALO = 32


def _conv_kernel(halo_ref, u_ref, wdw_ref, bdw_ref, g_ref, b_ref, o_ref, buf_ref, acc_ref):
    i = pl.program_id(0)
    tm, d = u_ref.shape
    cw = wdw_ref.shape[0]
    buf_ref[pl.ds(CONV_HALO, tm), :] = u_ref[...]
    buf_ref[pl.ds(0, CONV_HALO), :] = jnp.where(i > 0, halo_ref[...], 0.0)
    off = CONV_HALO - (cw - 1)

    def col_body(c, carry):
        c0 = pl.multiple_of(c * LANES, LANES)
        acc = jnp.zeros((tm, LANES), F32)
        window = buf_ref[:, pl.ds(c0, LANES)]
        n_buf = window.shape[0]
        for b in range(SUBLANES):
            taps = [(a, SUBLANES * a + b - off) for a in range((off + cw - 1) // SUBLANES + 1)
                    if 0 <= SUBLANES * a + b - off < cw]
            shifted = window if b == 0 else pltpu.roll(window, n_buf - b, axis=0)
            for a, w in taps:
                acc = acc + shifted[SUBLANES * a:SUBLANES * a + tm, :] * wdw_ref[pl.ds(w, 1), pl.ds(c0, LANES)]
        acc_ref[:, pl.ds(c0, LANES)] = acc + bdw_ref[:, pl.ds(c0, LANES)]
        return carry

    lax.fori_loop(0, d // LANES, col_body, 0)
    y = acc_ref[...]
    mu = jnp.mean(y, axis=-1, keepdims=True)
    yc = y - mu
    var = jnp.mean(yc * yc, axis=-1, keepdims=True)
    z = yc * lax.rsqrt(var + EPS) * g_ref[...] + b_ref[...]
    o_ref[...] = _silu(z).astype(o_ref.dtype)


def _conv_ln_silu(u, w_dw, b_dw, ln_g, ln_b):
    s, d = u.shape
    cw = w_dw.shape[0]
    assert cw - 1 <= CONV_HALO
    tm = _tile(s, CONV_ROWS, CONV_HALO)
    ratio = tm // CONV_HALO
    vec = pl.BlockSpec((1, d), lambda i: (0, 0))
    return pl.pallas_call(
        _conv_kernel,
        grid=(s // tm,),
        in_specs=[pl.BlockSpec((CONV_HALO, d), lambda i: (jnp.maximum(i * ratio - 1, 0), 0)),
                  pl.BlockSpec((tm, d), lambda i: (i, 0)),
                  pl.BlockSpec((cw, d), lambda i: (0, 0)),
                  vec, vec, vec],
        out_specs=pl.BlockSpec((tm, d), lambda i: (i, 0)),
        out_shape=jax.ShapeDtypeStruct((s, d), BF16),
        scratch_shapes=[pltpu.VMEM((CONV_HALO + tm, d), F32), pltpu.VMEM((tm, d), F32)],
        compiler_params=_params("arbitrary"),
        name="conv_ln_silu",
    )(u, u, w_dw, b_dw.reshape(1, d), ln_g.reshape(1, d), ln_b.reshape(1, d))


def _compress_kernel(c_ref, p_ref, w1_ref, b1_ref, w2_ref, b2_ref, g_ref, cos_ref, sin_ref, o_ref):
    j = pl.program_id(0)
    nch, half = c_ref.shape
    c = c_ref[...]
    a_lo = (c + p_ref[0:1, :]).astype(BF16)
    a_hi = (c + p_ref[1:2, :]).astype(BF16)
    h_lo = _dot(a_lo, w1_ref[pl.ds(0, half), :])
    h_hi = _dot(a_hi, w1_ref[pl.ds(half, half), :])
    hid = h_lo + pltpu.roll(h_hi, nch - 1, axis=0) + b1_ref[...]
    y = _dot(_silu(hid).astype(BF16), w2_ref[...]) + b2_ref[...]

    @pl.when(j == 0)
    def _():
        o_ref[...] = _rms_rope(y, g_ref[...], cos_ref[...], sin_ref[...]).astype(o_ref.dtype)

    @pl.when(j != 0)
    def _():
        o_ref[...] = y.astype(o_ref.dtype)


def _compress(raw, cmp_pos, w1, b1, w2, b2, g, cos_c, sin_c):
    _, ng, s, dh = raw.shape
    nch = s // CMP_STRIDE
    half = CMP_STRIDE * dh
    hid = w1.shape[-1]
    chunks = raw.reshape(2, ng, nch, half)
    pos2 = cmp_pos.reshape(2, CMP_LEN // CMP_STRIDE, half)
    sq = pl.Squeezed()
    return pl.pallas_call(
        _compress_kernel,
        grid=(2, ng),
        in_specs=[pl.BlockSpec((sq, sq, nch, half), lambda j, gi: (j, gi, 0, 0)),
                  pl.BlockSpec((sq, CMP_LEN // CMP_STRIDE, half), lambda j, gi: (j, 0, 0)),
                  pl.BlockSpec((sq, 2 * half, hid), lambda j, gi: (j, 0, 0)),
                  pl.BlockSpec((sq, 1, hid), lambda j, gi: (j, 0, 0)),
                  pl.BlockSpec((sq, hid, dh), lambda j, gi: (j, 0, 0)),
                  pl.BlockSpec((sq, 1, dh), lambda j, gi: (j, 0, 0)),
                  pl.BlockSpec((1, dh), lambda j, gi: (0, 0)),
                  pl.BlockSpec((nch, dh), lambda j, gi: (0, 0)),
                  pl.BlockSpec((nch, dh), lambda j, gi: (0, 0))],
        out_specs=pl.BlockSpec((sq, sq, nch, dh), lambda j, gi: (j, gi, 0, 0)),
        out_shape=jax.ShapeDtypeStruct((2, ng, nch, dh), BF16),
        compiler_params=_params("arbitrary", "arbitrary"),
        name="compress",
    )(chunks, pos2, w1, b1.reshape(2, 1, hid), w2, b2.reshape(2, 1, dh), g.reshape(1, dh), cos_c, sin_c)


NSA_TQ = 128
NSA_TK = 512


def _nsa_kernel(q_ref, kc_ref, vc_ref, ksa_ref, vsa_ref, kw_ref, vw_ref, gate_ref, mt_ref, o_ref,
                cnt_ref, qa_ref, sa_ref, sb_ref, pc_ref, pw_ref, ps_ref, m_ref, alpha_ref, acc_ref, out_ref):
    r, tq, dh = q_ref.shape
    rows = r * tq
    nc = kc_ref.shape[0]
    ns = mt_ref.shape[0]
    s_len = kw_ref.shape[0]
    n_aug = ksa_ref.shape[1] - dh
    tk = min(NSA_TK, s_len)
    wk = min(WINDOW + tq, s_len)
    qt = pl.program_id(1)
    t0 = qt * tq
    q2 = q_ref[...].reshape(rows, dh)
    t_q = t0 + lax.broadcasted_iota(jnp.int32, (tq, 1), 0)
    gates = gate_ref[...]
    heads = [slice(h * tq, (h + 1) * tq) for h in range(r)]

    k_start = pl.multiple_of(jnp.maximum(jnp.minimum(t0 + tq, s_len) - wk, 0), SUBLANES * 2)
    sa_ref[:, :nc] = _dot_nt(q2, kc_ref[...])
    sb_ref[:, :wk] = _dot_nt(q2, kw_ref[pl.ds(k_start, wk), :])

    cmp_end = lax.broadcasted_iota(jnp.int32, (1, nc), 1) * CMP_STRIDE + (CMP_LEN - 1)
    bias_c = jnp.where(cmp_end <= t_q, 0.0, NEG_INF)
    has_block = t_q >= CMP_LEN - 1
    p_sum = jnp.zeros((tq, nc), F32)
    for rs in heads:
        s = sa_ref[rs, :nc] + bias_c
        e = jnp.exp2(s - jnp.max(s, axis=-1, keepdims=True))
        l = jnp.sum(e, axis=-1, keepdims=True)
        p = e * jnp.where(has_block, 1.0 / l, 0.0)
        p_sum = p_sum + p
        pc_ref[rs, :] = p.astype(BF16)

    back = t_q - (k_start + lax.broadcasted_iota(jnp.int32, (1, wk), 1))
    bias_w = jnp.where((back >= 0) & (back < WINDOW), 0.0, NEG_INF)
    for rs in heads:
        s = sb_ref[rs, :wk] + bias_w
        e = jnp.exp2(s - jnp.max(s, axis=-1, keepdims=True))
        pw_ref[rs, :] = (e * (1.0 / jnp.sum(e, axis=-1, keepdims=True))).astype(BF16)

    o_c = _dot(pc_ref[...], vc_ref[...])
    o_w = _dot(pw_ref[...], vw_ref[pl.ds(k_start, wk), :])
    for h, rs in enumerate(heads):
        out_ref[rs, :] = gates[:, h:h + 1] * o_c[rs, :] + gates[:, 2 * r + h:2 * r + h + 1] * o_w[rs, :]

    p_hi = p_sum.astype(BF16)
    rem = p_sum - p_hi.astype(F32)
    p_mid = rem.astype(BF16)
    p_lo = (rem - p_mid.astype(F32)).astype(BF16)
    mt = mt_ref[...]
    imp = _dot_nt(mt, p_hi) + _dot_nt(mt, p_mid) + _dot_nt(mt, p_lo)

    t_row = t0 + lax.broadcasted_iota(jnp.int32, (1, tq), 1)
    blk = lax.broadcasted_iota(jnp.int32, (ns, 1), 0)
    dist = t_row // SLC_BLOCK - blk
    forced = (blk < N_INIT_BLOCKS) | ((dist >= 0) & (dist < N_LOCAL_BLOCKS))
    imp = jnp.where(forced, jnp.inf, jnp.where(dist >= 0, imp, -jnp.inf))

    n_rivals = jnp.minimum((t0 + tq - 1) // SLC_BLOCK + 1, ns)
    cnt_ref[...] = jnp.zeros(cnt_ref.shape, F32)
    for c in range(ns // SUBLANES):
        lo = c * SUBLANES

        @pl.when(lo < n_rivals)
        def _():
            below, own, above = imp[:lo], imp[lo:lo + SUBLANES], imp[lo + SUBLANES:]
            own_idx = lo + lax.broadcasted_iota(jnp.int32, (SUBLANES, 1), 0)
            cnt = cnt_ref[...]
            for i in range(lo, lo + SUBLANES):
                row = imp[i:i + 1, :]
                parts = [jnp.where(row > below, 1.0, 0.0),
                         jnp.where((row > own) | ((row == own) & (i < own_idx)), 1.0, 0.0),
                         jnp.where(row >= above, 1.0, 0.0)]
                cnt = cnt + jnp.concatenate([p for p in parts if p.shape[0]], axis=0)
            cnt_ref[...] = cnt

    chosen = (cnt_ref[...] < float(min(SLC_TOPK, ns))) & (dist >= 0)
    not_sel = jnp.where(chosen, 0.0, 1.0).T.astype(BF16)
    if n_aug > ns:
        not_sel = jnp.concatenate([not_sel, jnp.zeros((tq, n_aug - ns), BF16)], axis=1)
    qa_ref[:, :dh] = q2
    for rs in heads:
        qa_ref[rs, dh:] = not_sel

    m_ref[...] = jnp.full(m_ref.shape, NEG_INF, F32)
    acc_ref[...] = jnp.zeros(acc_ref.shape, F32)
    n_tiles = (t0 + tq + tk - 1) // tk
    last_tile = s_len // tk - 1

    def scores(kt):
        k0 = pl.multiple_of(jnp.minimum(kt, last_tile) * tk, tk)
        return _dot_nt(qa_ref[...], ksa_ref[pl.ds(k0, tk), :])

    def absorb(s_ref, kt):
        kpos = kt * tk + lax.broadcasted_iota(jnp.int32, (1, tk), 1)
        bias = jnp.where(kpos <= t_q, 0.0, NEG_INF)
        for rs in heads:
            s = s_ref[rs, :tk] + bias
            m_old = m_ref[rs, :]
            m_new = jnp.maximum(m_old, jnp.max(s, axis=-1, keepdims=True))
            ps_ref[rs, :] = jnp.exp2(s - _lane_tile(m_new, tk)).astype(BF16)
            alpha_ref[rs, :] = jnp.exp2(m_old - m_new)
            m_ref[rs, :] = m_new
        k0 = pl.multiple_of(jnp.minimum(kt, last_tile) * tk, tk)
        pv = _dot(ps_ref[...], vsa_ref[pl.ds(k0, tk), :])
        acc_ref[...] = acc_ref[...] * _lane_tile(alpha_ref[...], acc_ref.shape[1]) + pv

    sa_ref[:, :tk] = scores(0)

    def pair_body(j, carry):
        kt = 2 * j
        sb_ref[:, :tk] = scores(kt + 1)
        absorb(sa_ref, kt)
        sa_ref[:, :tk] = scores(kt + 2)
        absorb(sb_ref, kt + 1)
        return carry

    lax.fori_loop(0, n_tiles // 2, pair_body, 0)

    @pl.when(n_tiles % 2 == 1)
    def _():
        absorb(sa_ref, n_tiles - 1)

    for h, rs in enumerate(heads):
        o_s = acc_ref[rs, :dh] / acc_ref[rs, dh:2 * dh]
        o_ref[:, h * dh:(h + 1) * dh] = (out_ref[rs, :] + gates[:, r + h:r + h + 1] * o_s).astype(o_ref.dtype)


def _cmp_to_slc_t(n_chunks, n_slc):
    n_cmp = n_chunks - 1
    r, cl = SLC_BLOCK // CMP_STRIDE, CMP_LEN // CMP_STRIDE
    offs = (np.arange(r)[:, None] - np.arange(cl)[None, :]).reshape(-1)
    tgt = r * np.arange(n_slc)[None, :, None] + offs[None, None, :]
    m = (np.arange(n_cmp)[:, None, None] == tgt).sum(-1).astype(np.float32)
    out = np.zeros((n_slc, n_chunks), np.float32)
    out[:, :n_cmp] = m.T
    return out


def _augment_selected_kv(k_slc, v_slc):
    ng, s, dh = k_slc.shape
    ns = s // SLC_BLOCK
    n_aug = _round_up(ns, LANES)
    own_block = (np.arange(s)[:, None] // SLC_BLOCK) == np.arange(n_aug)[None, :]
    marker = jnp.asarray(np.where(own_block, NEG_INF, 0.0), BF16)
    k_aug = jnp.concatenate([k_slc, jnp.broadcast_to(marker, (ng, s, n_aug))], axis=-1)
    v_aug = jnp.concatenate([v_slc, jnp.ones((ng, s, dh), BF16)], axis=-1)
    return k_aug, v_aug


def _nsa_attention(q, k_cmp, v_cmp, k_aug, v_aug, k_win, v_win, gates):
    n_heads, s, dh = q.shape
    ng = k_win.shape[0]
    r = n_heads // ng
    nc = k_cmp.shape[1]
    ns = s // SLC_BLOCK
    tq = _tile(s, NSA_TQ, SLC_BLOCK)
    rows = r * tq
    tk = min(NSA_TK, s)
    wk = min(WINDOW + tq, s)
    width = max(nc, tk, wk)
    mt = jnp.asarray(_cmp_to_slc_t(nc, ns), BF16)
    sq = pl.Squeezed()
    per_group = lambda arr: pl.BlockSpec((sq,) + arr.shape[1:], lambda gi, qi: (gi, 0, 0))
    return pl.pallas_call(
        _nsa_kernel,
        grid=(ng, s // tq),
        in_specs=[pl.BlockSpec((r, tq, dh), lambda gi, qi: (gi, qi, 0)),
                  per_group(k_cmp), per_group(v_cmp), per_group(k_aug), per_group(v_aug),
                  per_group(k_win), per_group(v_win),
                  pl.BlockSpec((sq, tq, 3 * r), lambda gi, qi: (gi, qi, 0)),
                  pl.BlockSpec((ns, nc), lambda gi, qi: (0, 0))],
        out_specs=pl.BlockSpec((tq, r * dh), lambda gi, qi: (qi, gi)),
        out_shape=jax.ShapeDtypeStruct((s, n_heads * dh), BF16),
        scratch_shapes=[pltpu.VMEM((ns, tq), F32),
                        pltpu.VMEM((rows, k_aug.shape[2]), BF16),
                        pltpu.VMEM((rows, width), F32),
                        pltpu.VMEM((rows, width), F32),
                        pltpu.VMEM((rows, nc), BF16),
                        pltpu.VMEM((rows, wk), BF16),
                        pltpu.VMEM((rows, tk), BF16),
                        pltpu.VMEM((rows, LANES), F32),
                        pltpu.VMEM((rows, LANES), F32),
                        pltpu.VMEM((rows, v_aug.shape[2]), F32),
                        pltpu.VMEM((rows, dh), F32)],
        compiler_params=_params("arbitrary", "arbitrary"),
        name="nsa_attention",
    )(q, k_cmp, v_cmp, k_aug, v_aug, k_win, v_win, gates, mt)


def kernel(x, c, positions, w_ada, b_ada, ada_emb, kv_ada_emb, norm_mix, norm_ffn, norm_kv, conv_w_pw1, conv_b_pw1, conv_w_dw, conv_b_dw, conv_ln_g, conv_ln_b, conv_w_pw2, conv_b_pw2, nsa_w_q, nsa_q_norm, nsa_w_gate, nsa_b_gate, nsa_w_o, kv_w, kv_k_norm, cmp_pos, cmp_w1, cmp_b1, cmp_w2, cmp_b2, ffn_w_gu, ffn_w_down):
    batch, s, d = x.shape
    assert batch == 1, "the kernels are written for a single sequence"
    depth = ada_emb.shape[0]
    n_conv = conv_w_pw1.shape[0]
    ng = N_KV_GROUPS
    n_heads = d // HEAD_DIM
    r = n_heads // ng
    kvw = ng * HEAD_DIM
    zeros_d = jnp.zeros((d,), F32)

    xs = x[0]
    mod = _ada_proj(c, w_ada, b_ada).reshape(6, d)
    pos = positions[0]
    cos, sin = _rope_tables(pos)
    kv = None
    w_down = ffn_w_down.astype(BF16)
    for layer in range(depth):
        m = mod + ada_emb[layer]
        h = _prenorm(xs, norm_mix[layer], m[0], m[1])
        if layer < n_conv:
            u = _mm_glu(h, conv_w_pw1, layer, conv_b_pw1[layer])
            v = _conv_ln_silu(u, conv_w_dw[layer], conv_b_dw[layer], conv_ln_g[layer], conv_ln_b[layer])
            xs = _mm_resid(v, conv_w_pw2, layer, conv_b_pw2[layer], m[2], xs)
        else:
            if kv is None:
                h_kv = _prenorm(xs, norm_kv, mod[0] + kv_ada_emb[0], mod[1] + kv_ada_emb[1])
                kv_w3 = kv_w[None]
                raw = _mm_heads(h_kv, kv_w3, 0, 0, 2 * kvw, kv_k_norm, lambda j: 0, cos, sin,
                                rope_every=0, out_scale=1.0, out_dtype=F32)
                rest = _mm_heads(h_kv, kv_w3, 0, 2 * kvw, 4 * kvw, kv_k_norm, lambda j: 1 + j // 2, cos, sin,
                                 rope_every=2, out_scale=1.0, out_dtype=BF16)
                rest = rest.reshape(4, ng, s, HEAD_DIM)
                n_chunks = s // CMP_STRIDE
                pos_c = jnp.pad(pos[CMP_LEN - 1::CMP_STRIDE], (0, 1))[:n_chunks]
                cos_c, sin_c = _rope_tables(pos_c)
                cmp = _compress(raw.reshape(2, ng, s, HEAD_DIM), cmp_pos, cmp_w1.astype(BF16), cmp_b1,
                                cmp_w2.astype(BF16), cmp_b2, kv_k_norm[0], cos_c, sin_c)
                k_aug, v_aug = _augment_selected_kv(rest[0], rest[1])
                kv = (cmp[0], cmp[1], k_aug, v_aug, rest[2], rest[3])
            i = layer - n_conv
            q = _mm_heads(h, nsa_w_q, i, 0, n_heads * HEAD_DIM, nsa_q_norm[i:i + 1],
                          lambda j: 0, cos, sin, rope_every=1, out_scale=HEAD_DIM ** -0.5 * LOG2_E,
                          out_dtype=BF16)
            gates = _mm_sigmoid(h, nsa_w_gate, i, nsa_b_gate[i])
            gates = gates.reshape(s, 3, ng, r).transpose(2, 0, 1, 3).reshape(ng, s, 3 * r)
            o = _nsa_attention(q, *kv, gates)
            xs = _mm_resid(o, nsa_w_o, i, zeros_d, m[2], xs)
        h = _prenorm(xs, norm_ffn[layer], m[3], m[4])
        a = _mm_swiglu(h, ffn_w_gu, layer)
        xs = _mm_resid(a, w_down, layer, zeros_d, m[5], xs)
    return xs[None]
```

```python
import functools
import math

import numpy as np
import jax
import jax.numpy as jnp
from jax import lax
from jax.experimental import pallas as pl
from jax.experimental.pallas import tpu as pltpu

HEAD_DIM = 128
N_KV_GROUPS = 4
CMP_LEN = 32
CMP_STRIDE = 16
SLC_BLOCK = 64
SLC_TOPK = 16
N_INIT_BLOCKS = 1
N_LOCAL_BLOCKS = 2
WINDOW = 512
ROPE_THETA = 10000.0
EPS = 1e-6
NEG_INF = -1e30
LOG2_E = math.log2(math.e)

LANES = 128
SUBLANES = 8
VMEM_LIMIT_BYTES = 56 * 1024 * 1024

BF16 = jnp.bfloat16
F32 = jnp.float32


def _params(*sem):
    return pltpu.CompilerParams(dimension_semantics=sem, vmem_limit_bytes=VMEM_LIMIT_BYTES)


def _tile(n, target, quantum):
    if n <= target:
        return n
    t = (target // quantum) * quantum
    while t >= quantum:
        if n % t == 0:
            return t
        t -= quantum
    raise ValueError(f"no tile for {n} with quantum {quantum}")


def _round_up(n, q):
    return -(-n // q) * q


def _dot(a, b):
    return jnp.dot(a, b, preferred_element_type=F32)


def _dot_nt(a, b):
    return lax.dot_general(a, b, (((1,), (1,)), ((), ())), preferred_element_type=F32)


def _sigmoid(v):
    return 1.0 / (1.0 + jnp.exp(-v))


def _silu(v):
    return v * _sigmoid(v)


def _lane_tile(v, width):
    reps = width // v.shape[1]
    return v if reps == 1 else jnp.concatenate([v] * reps, axis=1)


def _ada_kernel(cb_ref, w_ref, b_ref, o_ref):
    d, tn = w_ref.shape
    n_chunks = tn // LANES

    def body(k, accs):
        r = pl.multiple_of(k * SUBLANES, SUBLANES)
        cv = _silu(cb_ref[pl.ds(r, SUBLANES), :])
        return tuple(
            accs[j] + cv * w_ref[pl.ds(r, SUBLANES), j * LANES:(j + 1) * LANES]
            for j in range(n_chunks))

    accs = lax.fori_loop(0, d // SUBLANES, body,
                         tuple(jnp.zeros((SUBLANES, LANES), F32) for _ in range(n_chunks)), unroll=4)
    for j in range(n_chunks):
        o_ref[:, j * LANES:(j + 1) * LANES] = (
            jnp.sum(accs[j], axis=0, keepdims=True) + b_ref[:, j * LANES:(j + 1) * LANES])


def _ada_proj(c, w_ada, b_ada):
    d, n = w_ada.shape
    tn = _tile(n, 1024, LANES)
    cb = jnp.broadcast_to(c.reshape(d, 1), (d, LANES))
    return pl.pallas_call(
        _ada_kernel,
        grid=(n // tn,),
        in_specs=[pl.BlockSpec((d, LANES), lambda j: (0, 0)),
                  pl.BlockSpec((d, tn), lambda j: (0, j)),
                  pl.BlockSpec((1, tn), lambda j: (0, j))],
        out_specs=pl.BlockSpec((1, tn), lambda j: (0, j)),
        out_shape=jax.ShapeDtypeStruct((1, n), F32),
        compiler_params=_params("arbitrary"),
        name="ada_proj",
    )(cb, w_ada, b_ada.reshape(1, n))


def _rope_table_kernel(pos_ref, inv_ref, cos_ref, sin_ref):
    ang = pos_ref[...].astype(F32) * inv_ref[...]
    lane = lax.broadcasted_iota(jnp.int32, ang.shape, 1)
    cos_ref[...] = jnp.cos(ang)
    s = jnp.sin(ang)
    sin_ref[...] = jnp.where(lane < HEAD_DIM // 2, -s, s)


def _rope_tables(pos):
    n = pos.shape[0]
    half = HEAD_DIM // 2
    inv = ROPE_THETA ** (-jnp.arange(half, dtype=F32) / half)
    inv2 = jnp.concatenate([inv, inv]).reshape(1, HEAD_DIM)
    tm = _tile(n, 1024, SUBLANES)
    return pl.pallas_call(
        _rope_table_kernel,
        grid=(n // tm,),
        in_specs=[pl.BlockSpec((tm, 1), lambda i: (i, 0)),
                  pl.BlockSpec((1, HEAD_DIM), lambda i: (0, 0))],
        out_specs=[pl.BlockSpec((tm, HEAD_DIM), lambda i: (i, 0)),
                   pl.BlockSpec((tm, HEAD_DIM), lambda i: (i, 0))],
        out_shape=[jax.ShapeDtypeStruct((n, HEAD_DIM), F32)] * 2,
        compiler_params=_params("arbitrary"),
        name="rope_tables",
    )(pos.reshape(n, 1), inv2)


def _rms_rope(v, g, cos, sin):
    vg = v * g
    rotated = vg * cos + pltpu.roll(vg, HEAD_DIM // 2, axis=1) * sin
    return rotated * lax.rsqrt(jnp.mean(v * v, axis=-1, keepdims=True) + EPS)


def _prenorm_kernel(x_ref, g_ref, shift_ref, scale_ref, o_ref):
    x = x_ref[...]
    y = x * lax.rsqrt(jnp.mean(x * x, axis=-1, keepdims=True) + EPS) * g_ref[...]
    o_ref[...] = (y * (1.0 + scale_ref[...]) + shift_ref[...]).astype(o_ref.dtype)


def _prenorm(x, g, shift, scale):
    s, d = x.shape
    tm = _tile(s, 512, SUBLANES)
    vec = pl.BlockSpec((1, d), lambda i: (0, 0))
    return pl.pallas_call(
        _prenorm_kernel,
        grid=(s // tm,),
        in_specs=[pl.BlockSpec((tm, d), lambda i: (i, 0)), vec, vec, vec],
        out_specs=pl.BlockSpec((tm, d), lambda i: (i, 0)),
        out_shape=jax.ShapeDtypeStruct((s, d), BF16),
        compiler_params=_params("arbitrary"),
        name="prenorm",
    )(x, g.reshape(1, d), shift.reshape(1, d), scale.reshape(1, d))


def _wblock(w_ref):
    w = w_ref[...]
    return w if w.dtype == BF16 else w.astype(BF16)


def _wspec(w, tn, layer, col_block):
    return pl.BlockSpec((pl.Squeezed(), w.shape[1], tn), lambda i, j: (layer, 0, col_block(j)))


def _mm_glu_kernel(a_ref, w1_ref, w2_ref, b1_ref, b2_ref, o_ref):
    a = a_ref[...]
    lin = _dot(a, _wblock(w1_ref)) + b1_ref[...]
    gate = _dot(a, _wblock(w2_ref)) + b2_ref[...]
    o_ref[...] = (lin * _sigmoid(gate)).astype(o_ref.dtype)


def _mm_glu(a, w, layer, b):
    m, k = a.shape
    n = w.shape[2] // 2
    tm = _tile(m, 1024, SUBLANES)
    tn = _tile(n, 256, LANES)
    nj = n // tn
    b2d = b.reshape(1, 2 * n)
    return pl.pallas_call(
        _mm_glu_kernel,
        grid=(m // tm, nj),
        in_specs=[pl.BlockSpec((tm, k), lambda i, j: (i, 0)),
                  _wspec(w, tn, layer, lambda j: j),
                  _wspec(w, tn, layer, lambda j: j + nj),
                  pl.BlockSpec((1, tn), lambda i, j: (0, j)),
                  pl.BlockSpec((1, tn), lambda i, j: (0, j + nj))],
        out_specs=pl.BlockSpec((tm, tn), lambda i, j: (i, j)),
        out_shape=jax.ShapeDtypeStruct((m, n), F32),
        compiler_params=_params("arbitrary", "arbitrary"),
        name="mm_glu",
    )(a, w, w, b2d, b2d)


def _mm_swiglu_kernel(a_ref, wg_ref, wu_ref, wd_ref, o_ref, wd_out_ref):
    a = a_ref[...]
    o_ref[...] = (_silu(_dot(a, _wblock(wg_ref))) * _dot(a, _wblock(wu_ref))).astype(o_ref.dtype)
    wd_out_ref[...] = wd_ref[...].astype(wd_out_ref.dtype)


def _mm_swiglu(a, w, w_down, layer):
    m, k = a.shape
    n = w.shape[2] // 2
    d = w_down.shape[2]
    tm = _tile(m, 1024, SUBLANES)
    tn = _tile(n, 256, LANES)
    nj = n // tn
    slab, rem = divmod(n, (m // tm) * nj)
    assert rem == 0 and slab % (2 * SUBLANES) == 0, "w_down rows must split evenly over the grid steps"
    return pl.pallas_call(
        _mm_swiglu_kernel,
        grid=(m // tm, nj),
        in_specs=[pl.BlockSpec((tm, k), lambda i, j: (i, 0)),
                  _wspec(w, tn, layer, lambda j: j),
                  _wspec(w, tn, layer, lambda j: j + nj),
                  pl.BlockSpec((pl.Squeezed(), slab, d), lambda i, j: (layer, i * nj + j, 0))],
        out_specs=[pl.BlockSpec((tm, tn), lambda i, j: (i, j)),
                   pl.BlockSpec((slab, d), lambda i, j: (i * nj + j, 0))],
        out_shape=[jax.ShapeDtypeStruct((m, n), BF16), jax.ShapeDtypeStruct((n, d), BF16)],
        compiler_params=_params("arbitrary", "arbitrary"),
        name="mm_swiglu",
    )(a, w, w, w_down)


def _mm_resid_kernel(a_ref, w_ref, bias_ref, gate_ref, x_ref, o_ref):
    y = _dot(a_ref[...], _wblock(w_ref)) + bias_ref[...]
    o_ref[...] = x_ref[...] + gate_ref[...] * y


def _mm_resid(a, w, layer, bias, gate, x):
    m, k = a.shape
    n = w.shape[2]
    big_k = k > 8192
    tm = _tile(m, 512 if big_k else 1024, SUBLANES)
    tn = _tile(n, 256 if big_k else 512, LANES)
    vec = pl.BlockSpec((1, tn), lambda i, j: (0, j))
    return pl.pallas_call(
        _mm_resid_kernel,
        grid=(m // tm, n // tn),
        in_specs=[pl.BlockSpec((tm, k), lambda i, j: (i, 0)),
                  _wspec(w, tn, layer, lambda j: j),
                  vec, vec,
                  pl.BlockSpec((tm, tn), lambda i, j: (i, j))],
        out_specs=pl.BlockSpec((tm, tn), lambda i, j: (i, j)),
        out_shape=jax.ShapeDtypeStruct((m, n), F32),
        compiler_params=_params("arbitrary", "arbitrary"),
        name="mm_resid",
    )(a, w, bias.reshape(1, n), gate.reshape(1, n), x)


def _mm_sigmoid_kernel(a_ref, w_ref, b_ref, o_ref):
    o_ref[...] = _sigmoid(_dot(a_ref[...], _wblock(w_ref)) + b_ref[...])


def _mm_sigmoid(a, w, layer, b):
    m, k = a.shape
    n = w.shape[2]
    tm = _tile(m, 1024, SUBLANES)
    return pl.pallas_call(
        _mm_sigmoid_kernel,
        grid=(m // tm, 1),
        in_specs=[pl.BlockSpec((tm, k), lambda i, j: (i, 0)),
                  _wspec(w, n, layer, lambda j: 0),
                  pl.BlockSpec((1, n), lambda i, j: (0, 0))],
        out_specs=pl.BlockSpec((tm, n), lambda i, j: (i, 0)),
        out_shape=jax.ShapeDtypeStruct((m, n), F32),
        compiler_params=_params("arbitrary", "arbitrary"),
        name="mm_sigmoid",
    )(a, w, b.reshape(1, n))


HEADS_CHUNK = 128


def _mm_heads_kernel(a_ref, w_ref, g_ref, cos_ref, sin_ref, o_ref, y_ref, *, rope_every, out_scale):
    j = pl.program_id(1)
    nh, tm, _ = o_ref.shape
    y = _dot(a_ref[...], _wblock(w_ref))

    def plain():
        for h in range(nh):
            o_ref[h] = y[:, h * HEAD_DIM:(h + 1) * HEAD_DIM].astype(o_ref.dtype)

    def roped():
        y_ref[...] = y
        g = g_ref[0]
        chunk = min(HEADS_CHUNK, tm)

        def body(c, carry):
            rs = pl.ds(pl.multiple_of(c * chunk, chunk), chunk)
            cos, sin = cos_ref[rs, :], sin_ref[rs, :]
            for h in range(nh):
                v = _rms_rope(y_ref[rs, h * HEAD_DIM:(h + 1) * HEAD_DIM], g, cos, sin)
                o_ref[h, rs, :] = (v * out_scale).astype(o_ref.dtype)
            return carry

        lax.fori_loop(0, tm // chunk, body, 0, unroll=2)

    if rope_every == 0:
        plain()
    elif rope_every == 1:
        roped()
    else:
        pl.when(j % rope_every == 0)(roped)
        pl.when(j % rope_every != 0)(plain)


def _mm_heads(a, w, layer, col0, n_cols, gains, gain_of_block, cos, sin, *, rope_every, out_scale, out_dtype):
    m, k = a.shape
    tn = 4 * HEAD_DIM
    nh = tn // HEAD_DIM
    tm = _tile(m, 1024, SUBLANES)
    j0 = col0 // tn
    ng = gains.shape[0]
    return pl.pallas_call(
        functools.partial(_mm_heads_kernel, rope_every=rope_every, out_scale=out_scale),
        grid=(m // tm, n_cols // tn),
        in_specs=[pl.BlockSpec((tm, k), lambda i, j: (i, 0)),
                  _wspec(w, tn, layer, lambda j: j + j0),
                  pl.BlockSpec((1, 1, HEAD_DIM), lambda i, j: (gain_of_block(j), 0, 0)),
                  pl.BlockSpec((tm, HEAD_DIM), lambda i, j: (i, 0)),
                  pl.BlockSpec((tm, HEAD_DIM), lambda i, j: (i, 0))],
        out_specs=pl.BlockSpec((nh, tm, HEAD_DIM), lambda i, j: (j, i, 0)),
        out_shape=jax.ShapeDtypeStruct((n_cols // HEAD_DIM, m, HEAD_DIM), out_dtype),
        scratch_shapes=[pltpu.VMEM((tm, tn), F32)],
        compiler_params=_params("arbitrary", "arbitrary"),
        name="mm_heads",
    )(a, w, gains.reshape(ng, 1, HEAD_DIM), cos, sin)


CONV_ROWS = 64
CONV_HALO = 32


def _conv_kernel(halo_ref, u_ref, wdw_ref, bdw_ref, g_ref, b_ref, o_ref, buf_ref, acc_ref):
    i = pl.program_id(0)
    tm, d = u_ref.shape
    cw = wdw_ref.shape[0]
    buf_ref[pl.ds(CONV_HALO, tm), :] = u_ref[...]
    buf_ref[pl.ds(0, CONV_HALO), :] = jnp.where(i > 0, halo_ref[...], 0.0)
    off = CONV_HALO - (cw - 1)

    def col_body(c, carry):
        c0 = pl.multiple_of(c * LANES, LANES)
        acc = jnp.zeros((tm, LANES), F32)
        window = buf_ref[:, pl.ds(c0, LANES)]
        n_buf = window.shape[0]
        for b in range(SUBLANES):
            taps = [(a, SUBLANES * a + b - off) for a in range((off + cw - 1) // SUBLANES + 1)
                    if 0 <= SUBLANES * a + b - off < cw]
            shifted = window if b == 0 else pltpu.roll(window, n_buf - b, axis=0)
            for a, w in taps:
                acc = acc + shifted[SUBLANES * a:SUBLANES * a + tm, :] * wdw_ref[pl.ds(w, 1), pl.ds(c0, LANES)]
        acc_ref[:, pl.ds(c0, LANES)] = acc + bdw_ref[:, pl.ds(c0, LANES)]
        return carry

    lax.fori_loop(0, d // LANES, col_body, 0)
    y = acc_ref[...]
    mu = jnp.mean(y, axis=-1, keepdims=True)
    yc = y - mu
    var = jnp.mean(yc * yc, axis=-1, keepdims=True)
    z = yc * lax.rsqrt(var + EPS) * g_ref[...] + b_ref[...]
    o_ref[...] = _silu(z).astype(o_ref.dtype)


def _conv_ln_silu(u, w_dw, b_dw, ln_g, ln_b):
    s, d = u.shape
    cw = w_dw.shape[0]
    assert cw - 1 <= CONV_HALO
    tm = _tile(s, CONV_ROWS, CONV_HALO)
    ratio = tm // CONV_HALO
    vec = pl.BlockSpec((1, d), lambda i: (0, 0))
    return pl.pallas_call(
        _conv_kernel,
        grid=(s // tm,),
        in_specs=[pl.BlockSpec((CONV_HALO, d), lambda i: (jnp.maximum(i * ratio - 1, 0), 0)),
                  pl.BlockSpec((tm, d), lambda i: (i, 0)),
                  pl.BlockSpec((cw, d), lambda i: (0, 0)),
                  vec, vec, vec],
        out_specs=pl.BlockSpec((tm, d), lambda i: (i, 0)),
        out_shape=jax.ShapeDtypeStruct((s, d), BF16),
        scratch_shapes=[pltpu.VMEM((CONV_HALO + tm, d), F32), pltpu.VMEM((tm, d), F32)],
        compiler_params=_params("arbitrary"),
        name="conv_ln_silu",
    )(u, u, w_dw, b_dw.reshape(1, d), ln_g.reshape(1, d), ln_b.reshape(1, d))


def _compress_kernel(c_ref, p_ref, w1_ref, b1_ref, w2_ref, b2_ref, g_ref, cos_ref, sin_ref, o_ref):
    j = pl.program_id(0)
    nch, half = c_ref.shape
    c = c_ref[...]
    a_lo = (c + p_ref[0:1, :]).astype(BF16)
    a_hi = (c + p_ref[1:2, :]).astype(BF16)
    h_lo = _dot(a_lo, w1_ref[pl.ds(0, half), :])
    h_hi = _dot(a_hi, w1_ref[pl.ds(half, half), :])
    hid = h_lo + pltpu.roll(h_hi, nch - 1, axis=0) + b1_ref[...]
    y = _dot(_silu(hid).astype(BF16), w2_ref[...]) + b2_ref[...]

    @pl.when(j == 0)
    def _():
        o_ref[...] = _rms_rope(y, g_ref[...], cos_ref[...], sin_ref[...]).astype(o_ref.dtype)

    @pl.when(j != 0)
    def _():
        o_ref[...] = y.astype(o_ref.dtype)


def _compress(raw, cmp_pos, w1, b1, w2, b2, g, cos_c, sin_c):
    _, ng, s, dh = raw.shape
    nch = s // CMP_STRIDE
    half = CMP_STRIDE * dh
    hid = w1.shape[-1]
    chunks = raw.reshape(2, ng, nch, half)
    pos2 = cmp_pos.reshape(2, CMP_LEN // CMP_STRIDE, half)
    sq = pl.Squeezed()
    return pl.pallas_call(
        _compress_kernel,
        grid=(2, ng),
        in_specs=[pl.BlockSpec((sq, sq, nch, half), lambda j, gi: (j, gi, 0, 0)),
                  pl.BlockSpec((sq, CMP_LEN // CMP_STRIDE, half), lambda j, gi: (j, 0, 0)),
                  pl.BlockSpec((sq, 2 * half, hid), lambda j, gi: (j, 0, 0)),
                  pl.BlockSpec((sq, 1, hid), lambda j, gi: (j, 0, 0)),
                  pl.BlockSpec((sq, hid, dh), lambda j, gi: (j, 0, 0)),
                  pl.BlockSpec((sq, 1, dh), lambda j, gi: (j, 0, 0)),
                  pl.BlockSpec((1, dh), lambda j, gi: (0, 0)),
                  pl.BlockSpec((nch, dh), lambda j, gi: (0, 0)),
                  pl.BlockSpec((nch, dh), lambda j, gi: (0, 0))],
        out_specs=pl.BlockSpec((sq, sq, nch, dh), lambda j, gi: (j, gi, 0, 0)),
        out_shape=jax.ShapeDtypeStruct((2, ng, nch, dh), BF16),
        compiler_params=_params("arbitrary", "arbitrary"),
        name="compress",
    )(chunks, pos2, w1, b1.reshape(2, 1, hid), w2, b2.reshape(2, 1, dh), g.reshape(1, dh), cos_c, sin_c)


NSA_TQ = 128
NSA_TK = 512


def _nsa_kernel(q_ref, kc_ref, vc_ref, ksa_ref, vsa_ref, kw_ref, vw_ref, gate_ref, mt_ref, o_ref,
                cnt_ref, qa_ref, sa_ref, sb_ref, pc_ref, pw_ref, ps_ref, m_ref, alpha_ref, acc_ref, out_ref):
    r, tq, dh = q_ref.shape
    rows = r * tq
    nc = kc_ref.shape[0]
    ns = mt_ref.shape[0]
    s_len = kw_ref.shape[0]
    n_aug = ksa_ref.shape[1] - dh
    tk = min(NSA_TK, s_len)
    wk = min(WINDOW + tq, s_len)
    qt = pl.program_id(1)
    t0 = qt * tq
    q2 = q_ref[...].reshape(rows, dh)
    t_q = t0 + lax.broadcasted_iota(jnp.int32, (tq, 1), 0)
    gates = gate_ref[...]
    heads = [slice(h * tq, (h + 1) * tq) for h in range(r)]

    k_start = pl.multiple_of(jnp.maximum(jnp.minimum(t0 + tq, s_len) - wk, 0), SUBLANES * 2)
    sa_ref[:, :nc] = _dot_nt(q2, kc_ref[...])
    sb_ref[:, :wk] = _dot_nt(q2, kw_ref[pl.ds(k_start, wk), :])

    cmp_end = lax.broadcasted_iota(jnp.int32, (1, nc), 1) * CMP_STRIDE + (CMP_LEN - 1)
    bias_c = jnp.where(cmp_end <= t_q, 0.0, NEG_INF)
    has_block = t_q >= CMP_LEN - 1
    p_sum = jnp.zeros((tq, nc), F32)
    for rs in heads:
        s = sa_ref[rs, :nc] + bias_c
        e = jnp.exp2(s - jnp.max(s, axis=-1, keepdims=True))
        l = jnp.sum(e, axis=-1, keepdims=True)
        p = e * jnp.where(has_block, 1.0 / l, 0.0)
        p_sum = p_sum + p
        pc_ref[rs, :] = p.astype(BF16)

    back = t_q - (k_start + lax.broadcasted_iota(jnp.int32, (1, wk), 1))
    bias_w = jnp.where((back >= 0) & (back < WINDOW), 0.0, NEG_INF)
    for rs in heads:
        s = sb_ref[rs, :wk] + bias_w
        pw_ref[rs, :] = jnp.exp2(s - jnp.max(s, axis=-1, keepdims=True)).astype(BF16)

    o_c = _dot(pc_ref[...], vc_ref[...])
    o_w = _dot(pw_ref[...], vw_ref[pl.ds(k_start, wk), :])
    for h, rs in enumerate(heads):
        out_ref[rs, :] = (gates[:, h:h + 1] * o_c[rs, :]
                          + gates[:, 2 * r + h:2 * r + h + 1] * (o_w[rs, :dh] / o_w[rs, dh:2 * dh]))

    p_hi = p_sum.astype(BF16)
    rem = p_sum - p_hi.astype(F32)
    p_mid = rem.astype(BF16)
    p_lo = (rem - p_mid.astype(F32)).astype(BF16)
    mt = mt_ref[...]
    imp = _dot_nt(mt, p_hi) + _dot_nt(mt, p_mid) + _dot_nt(mt, p_lo)

    t_row = t0 + lax.broadcasted_iota(jnp.int32, (1, tq), 1)
    blk = lax.broadcasted_iota(jnp.int32, (ns, 1), 0)
    dist = t_row // SLC_BLOCK - blk
    forced = (blk < N_INIT_BLOCKS) | ((dist >= 0) & (dist < N_LOCAL_BLOCKS))
    imp = jnp.where(forced, jnp.inf, jnp.where(dist >= 0, imp, -jnp.inf))

    n_rivals = jnp.minimum((t0 + tq - 1) // SLC_BLOCK + 1, ns)
    cnt_ref[...] = jnp.zeros(cnt_ref.shape, F32)
    for c in range(ns // SUBLANES):
        lo = c * SUBLANES

        @pl.when(lo < n_rivals)
        def _():
            below, own, above = imp[:lo], imp[lo:lo + SUBLANES], imp[lo + SUBLANES:]
            own_idx = lo + lax.broadcasted_iota(jnp.int32, (SUBLANES, 1), 0)
            cnt = cnt_ref[...]
            for i in range(lo, lo + SUBLANES):
                row = imp[i:i + 1, :]
                parts = [jnp.where(row > below, 1.0, 0.0),
                         jnp.where((row > own) | ((row == own) & (i < own_idx)), 1.0, 0.0),
                         jnp.where(row >= above, 1.0, 0.0)]
                cnt = cnt + jnp.concatenate([p for p in parts if p.shape[0]], axis=0)
            cnt_ref[...] = cnt

    chosen = (cnt_ref[...] < float(min(SLC_TOPK, ns))) & (dist >= 0)
    not_sel = jnp.where(chosen, 0.0, 1.0).T.astype(BF16)
    if n_aug > ns:
        not_sel = jnp.concatenate([not_sel, jnp.zeros((tq, n_aug - ns), BF16)], axis=1)
    qa_ref[:, :dh] = q2
    for rs in heads:
        qa_ref[rs, dh:] = not_sel

    m_ref[...] = jnp.full(m_ref.shape, NEG_INF, F32)
    acc_ref[...] = jnp.zeros(acc_ref.shape, F32)
    n_tiles = (t0 + tq + tk - 1) // tk
    last_tile = s_len // tk - 1

    def scores(kt):
        k0 = pl.multiple_of(jnp.minimum(kt, last_tile) * tk, tk)
        return _dot_nt(qa_ref[...], ksa_ref[pl.ds(k0, tk), :])

    def absorb(s_ref, kt):
        kpos = kt * tk + lax.broadcasted_iota(jnp.int32, (1, tk), 1)
        bias = jnp.where(kpos <= t_q, 0.0, NEG_INF)
        for rs in heads:
            s = s_ref[rs, :tk] + bias
            m_old = m_ref[rs, :]
            m_new = jnp.maximum(m_old, jnp.max(s, axis=-1, keepdims=True))
            ps_ref[rs, :] = jnp.exp2(s - _lane_tile(m_new, tk)).astype(BF16)
            alpha_ref[rs, :] = jnp.exp2(m_old - m_new)
            m_ref[rs, :] = m_new
        k0 = pl.multiple_of(jnp.minimum(kt, last_tile) * tk, tk)
        pv = _dot(ps_ref[...], vsa_ref[pl.ds(k0, tk), :])
        acc_ref[...] = acc_ref[...] * _lane_tile(alpha_ref[...], acc_ref.shape[1]) + pv

    sa_ref[:, :tk] = scores(0)

    def pair_body(j, carry):
        kt = 2 * j
        sb_ref[:, :tk] = scores(kt + 1)
        absorb(sa_ref, kt)
        sa_ref[:, :tk] = scores(kt + 2)
        absorb(sb_ref, kt + 1)
        return carry

    lax.fori_loop(0, n_tiles // 2, pair_body, 0)

    @pl.when(n_tiles % 2 == 1)
    def _():
        absorb(sa_ref, n_tiles - 1)

    for h, rs in enumerate(heads):
        o_s = acc_ref[rs, :dh] / acc_ref[rs, dh:2 * dh]
        o_ref[:, h * dh:(h + 1) * dh] = (out_ref[rs, :] + gates[:, r + h:r + h + 1] * o_s).astype(o_ref.dtype)


def _cmp_to_slc_t(n_chunks, n_slc):
    n_cmp = n_chunks - 1
    r, cl = SLC_BLOCK // CMP_STRIDE, CMP_LEN // CMP_STRIDE
    offs = (np.arange(r)[:, None] - np.arange(cl)[None, :]).reshape(-1)
    tgt = r * np.arange(n_slc)[None, :, None] + offs[None, None, :]
    m = (np.arange(n_cmp)[:, None, None] == tgt).sum(-1).astype(np.float32)
    out = np.zeros((n_slc, n_chunks), np.float32)
    out[:, :n_cmp] = m.T
    return out


def _with_block_markers(k_slc):
    ng, s, _ = k_slc.shape
    n_aug = _round_up(s // SLC_BLOCK, LANES)
    own_block = (np.arange(s)[:, None] // SLC_BLOCK) == np.arange(n_aug)[None, :]
    marker = jnp.asarray(np.where(own_block, NEG_INF, 0.0), BF16)
    return jnp.concatenate([k_slc, jnp.broadcast_to(marker, (ng, s, n_aug))], axis=-1)


def _with_ones(v):
    return jnp.concatenate([v, jnp.ones_like(v)], axis=-1)


def _nsa_attention(q, k_cmp, v_cmp, k_aug, v_aug, k_win, v_win, gates):
    n_heads, s, dh = q.shape
    ng = k_win.shape[0]
    r = n_heads // ng
    nc = k_cmp.shape[1]
    ns = s // SLC_BLOCK
    tq = _tile(s, NSA_TQ, SLC_BLOCK)
    rows = r * tq
    tk = min(NSA_TK, s)
    wk = min(WINDOW + tq, s)
    width = max(nc, tk, wk)
    mt = jnp.asarray(_cmp_to_slc_t(nc, ns), BF16)
    sq = pl.Squeezed()
    per_group = lambda arr: pl.BlockSpec((sq,) + arr.shape[1:], lambda gi, qi: (gi, 0, 0))
    return pl.pallas_call(
        _nsa_kernel,
        grid=(ng, s // tq),
        in_specs=[pl.BlockSpec((r, tq, dh), lambda gi, qi: (gi, qi, 0)),
                  per_group(k_cmp), per_group(v_cmp), per_group(k_aug), per_group(v_aug),
                  per_group(k_win), per_group(v_win),
                  pl.BlockSpec((sq, tq, 3 * r), lambda gi, qi: (gi, qi, 0)),
                  pl.BlockSpec((ns, nc), lambda gi, qi: (0, 0))],
        out_specs=pl.BlockSpec((tq, r * dh), lambda gi, qi: (qi, gi)),
        out_shape=jax.ShapeDtypeStruct((s, n_heads * dh), BF16),
        scratch_shapes=[pltpu.VMEM((ns, tq), F32),
                        pltpu.VMEM((rows, k_aug.shape[2]), BF16),
                        pltpu.VMEM((rows, width), F32),
                        pltpu.VMEM((rows, width), F32),
                        pltpu.VMEM((rows, nc), BF16),
                        pltpu.VMEM((rows, wk), BF16),
                        pltpu.VMEM((rows, tk), BF16),
                        pltpu.VMEM((rows, LANES), F32),
                        pltpu.VMEM((rows, LANES), F32),
                        pltpu.VMEM((rows, v_aug.shape[2]), F32),
                        pltpu.VMEM((rows, dh), F32)],
        compiler_params=_params("arbitrary", "arbitrary"),
        name="nsa_attention",
    )(q, k_cmp, v_cmp, k_aug, v_aug, k_win, v_win, gates, mt)


def kernel(x, c, positions, w_ada, b_ada, ada_emb, kv_ada_emb, norm_mix, norm_ffn, norm_kv, conv_w_pw1, conv_b_pw1, conv_w_dw, conv_b_dw, conv_ln_g, conv_ln_b, conv_w_pw2, conv_b_pw2, nsa_w_q, nsa_q_norm, nsa_w_gate, nsa_b_gate, nsa_w_o, kv_w, kv_k_norm, cmp_pos, cmp_w1, cmp_b1, cmp_w2, cmp_b2, ffn_w_gu, ffn_w_down):
    batch, s, d = x.shape
    assert batch == 1, "the kernels are written for a single sequence"
    depth = ada_emb.shape[0]
    n_conv = conv_w_pw1.shape[0]
    ng = N_KV_GROUPS
    n_heads = d // HEAD_DIM
    r = n_heads // ng
    kvw = ng * HEAD_DIM
    zeros_d = jnp.zeros((d,), F32)

    xs = x[0]
    mod = _ada_proj(c, w_ada, b_ada).reshape(6, d)
    pos = positions[0]
    cos, sin = _rope_tables(pos)
    kv = None
    for layer in range(depth):
        m = mod + ada_emb[layer]
        h = _prenorm(xs, norm_mix[layer], m[0], m[1])
        if layer < n_conv:
            u = _mm_glu(h, conv_w_pw1, layer, conv_b_pw1[layer])
            v = _conv_ln_silu(u, conv_w_dw[layer], conv_b_dw[layer], conv_ln_g[layer], conv_ln_b[layer])
            xs = _mm_resid(v, conv_w_pw2, layer, conv_b_pw2[layer], m[2], xs)
        else:
            if kv is None:
                h_kv = _prenorm(xs, norm_kv, mod[0] + kv_ada_emb[0], mod[1] + kv_ada_emb[1])
                kv_w3 = kv_w[None]
                raw = _mm_heads(h_kv, kv_w3, 0, 0, 2 * kvw, kv_k_norm, lambda j: 0, cos, sin,
                                rope_every=0, out_scale=1.0, out_dtype=F32)
                rest = _mm_heads(h_kv, kv_w3, 0, 2 * kvw, 4 * kvw, kv_k_norm, lambda j: 1 + j // 2, cos, sin,
                                 rope_every=2, out_scale=1.0, out_dtype=BF16)
                rest = rest.reshape(4, ng, s, HEAD_DIM)
                n_chunks = s // CMP_STRIDE
                pos_c = jnp.pad(pos[CMP_LEN - 1::CMP_STRIDE], (0, 1))[:n_chunks]
                cos_c, sin_c = _rope_tables(pos_c)
                cmp = _compress(raw.reshape(2, ng, s, HEAD_DIM), cmp_pos, cmp_w1.astype(BF16), cmp_b1,
                                cmp_w2.astype(BF16), cmp_b2, kv_k_norm[0], cos_c, sin_c)
                kv = (cmp[0], cmp[1], _with_block_markers(rest[0]), _with_ones(rest[1]),
                      rest[2], _with_ones(rest[3]))
            i = layer - n_conv
            q = _mm_heads(h, nsa_w_q, i, 0, n_heads * HEAD_DIM, nsa_q_norm[i:i + 1],
                          lambda j: 0, cos, sin, rope_every=1, out_scale=HEAD_DIM ** -0.5 * LOG2_E,
                          out_dtype=BF16)
            gates = _mm_sigmoid(h, nsa_w_gate, i, nsa_b_gate[i])
            gates = gates.reshape(s, 3, ng, r).transpose(2, 0, 1, 3).reshape(ng, s, 3 * r)
            o = _nsa_attention(q, *kv, gates)
            xs = _mm_resid(o, nsa_w_o, i, zeros_d, m[2], xs)
        h = _prenorm(xs, norm_ffn[layer], m[3], m[4])
        a, w_down = _mm_swiglu(h, ffn_w_gu, ffn_w_down, layer)
        xs = _mm_resid(a, w_down[None], 0, zeros_d, m[5], xs)
    return xs[None]
```

```python
import functools
import math

import numpy as np
import jax
import jax.numpy as jnp
from jax import lax
from jax.experimental import pallas as pl
from jax.experimental.pallas import tpu as pltpu

HEAD_DIM = 128
N_KV_GROUPS = 4
CMP_LEN = 32
CMP_STRIDE = 16
SLC_BLOCK = 64
SLC_TOPK = 16
N_INIT_BLOCKS = 1
N_LOCAL_BLOCKS = 2
WINDOW = 512
ROPE_THETA = 10000.0
EPS = 1e-6
NEG_INF = -1e30
LOG2_E = math.log2(math.e)

LANES = 128
SUBLANES = 8
VMEM_LIMIT_BYTES = 56 * 1024 * 1024

BF16 = jnp.bfloat16
F32 = jnp.float32


def _params(*sem):
    return pltpu.CompilerParams(dimension_semantics=sem, vmem_limit_bytes=VMEM_LIMIT_BYTES)


def _tile(n, target, quantum):
    if n <= target:
        return n
    t = (target // quantum) * quantum
    while t >= quantum:
        if n % t == 0:
            return t
        t -= quantum
    raise ValueError(f"no tile for {n} with quantum {quantum}")


def _round_up(n, q):
    return -(-n // q) * q


def _dot(a, b):
    return jnp.dot(a, b, preferred_element_type=F32)


def _dot_nt(a, b):
    return lax.dot_general(a, b, (((1,), (1,)), ((), ())), preferred_element_type=F32)


def _sigmoid(v):
    return 1.0 / (1.0 + jnp.exp(-v))


def _silu(v):
    return v * _sigmoid(v)


def _lane_tile(v, width):
    reps = width // v.shape[1]
    return v if reps == 1 else jnp.concatenate([v] * reps, axis=1)


def _ada_kernel(cb_ref, w_ref, b_ref, o_ref):
    d, tn = w_ref.shape
    n_chunks = tn // LANES

    def body(k, accs):
        r = pl.multiple_of(k * SUBLANES, SUBLANES)
        cv = _silu(cb_ref[pl.ds(r, SUBLANES), :])
        return tuple(
            accs[j] + cv * w_ref[pl.ds(r, SUBLANES), j * LANES:(j + 1) * LANES]
            for j in range(n_chunks))

    accs = lax.fori_loop(0, d // SUBLANES, body,
                         tuple(jnp.zeros((SUBLANES, LANES), F32) for _ in range(n_chunks)), unroll=4)
    for j in range(n_chunks):
        o_ref[:, j * LANES:(j + 1) * LANES] = (
            jnp.sum(accs[j], axis=0, keepdims=True) + b_ref[:, j * LANES:(j + 1) * LANES])


def _ada_proj(c, w_ada, b_ada):
    d, n = w_ada.shape
    tn = _tile(n, 1024, LANES)
    cb = jnp.broadcast_to(c.reshape(d, 1), (d, LANES))
    return pl.pallas_call(
        _ada_kernel,
        grid=(n // tn,),
        in_specs=[pl.BlockSpec((d, LANES), lambda j: (0, 0)),
                  pl.BlockSpec((d, tn), lambda j: (0, j)),
                  pl.BlockSpec((1, tn), lambda j: (0, j))],
        out_specs=pl.BlockSpec((1, tn), lambda j: (0, j)),
        out_shape=jax.ShapeDtypeStruct((1, n), F32),
        compiler_params=_params("arbitrary"),
        name="ada_proj",
    )(cb, w_ada, b_ada.reshape(1, n))


def _rope_table_kernel(pos_ref, inv_ref, cos_ref, sin_ref):
    ang = pos_ref[...].astype(F32) * inv_ref[...]
    lane = lax.broadcasted_iota(jnp.int32, ang.shape, 1)
    cos_ref[...] = jnp.cos(ang)
    s = jnp.sin(ang)
    sin_ref[...] = jnp.where(lane < HEAD_DIM // 2, -s, s)


def _rope_tables(pos):
    n = pos.shape[0]
    half = HEAD_DIM // 2
    inv = ROPE_THETA ** (-jnp.arange(half, dtype=F32) / half)
    inv2 = jnp.concatenate([inv, inv]).reshape(1, HEAD_DIM)
    tm = _tile(n, 1024, SUBLANES)
    return pl.pallas_call(
        _rope_table_kernel,
        grid=(n // tm,),
        in_specs=[pl.BlockSpec((tm, 1), lambda i: (i, 0)),
                  pl.BlockSpec((1, HEAD_DIM), lambda i: (0, 0))],
        out_specs=[pl.BlockSpec((tm, HEAD_DIM), lambda i: (i, 0)),
                   pl.BlockSpec((tm, HEAD_DIM), lambda i: (i, 0))],
        out_shape=[jax.ShapeDtypeStruct((n, HEAD_DIM), F32)] * 2,
        compiler_params=_params("arbitrary"),
        name="rope_tables",
    )(pos.reshape(n, 1), inv2)


def _rms_rope(v, g, cos, sin):
    vg = v * g
    rotated = vg * cos + pltpu.roll(vg, HEAD_DIM // 2, axis=1) * sin
    sq = v * v
    hi = sq.astype(BF16)
    lo = (sq - hi.astype(F32)).astype(BF16)
    ones = jnp.ones((HEAD_DIM, HEAD_DIM), BF16)
    sum_sq = _dot(hi, ones) + _dot(lo, ones)
    return rotated * lax.rsqrt(sum_sq * (1.0 / HEAD_DIM) + EPS)


def _prenorm_kernel(x_ref, g_ref, shift_ref, scale_ref, o_ref):
    x = x_ref[...]
    y = x * lax.rsqrt(jnp.mean(x * x, axis=-1, keepdims=True) + EPS) * g_ref[...]
    o_ref[...] = (y * (1.0 + scale_ref[...]) + shift_ref[...]).astype(o_ref.dtype)


def _prenorm(x, g, shift, scale):
    s, d = x.shape
    tm = _tile(s, 512, SUBLANES)
    vec = pl.BlockSpec((1, d), lambda i: (0, 0))
    return pl.pallas_call(
        _prenorm_kernel,
        grid=(s // tm,),
        in_specs=[pl.BlockSpec((tm, d), lambda i: (i, 0)), vec, vec, vec],
        out_specs=pl.BlockSpec((tm, d), lambda i: (i, 0)),
        out_shape=jax.ShapeDtypeStruct((s, d), BF16),
        compiler_params=_params("arbitrary"),
        name="prenorm",
    )(x, g.reshape(1, d), shift.reshape(1, d), scale.reshape(1, d))


def _wblock(w_ref):
    w = w_ref[...]
    return w if w.dtype == BF16 else w.astype(BF16)


def _wspec(w, tn, layer, col_block):
    return pl.BlockSpec((pl.Squeezed(), w.shape[1], tn), lambda i, j: (layer, 0, col_block(j)))


def _mm_glu_kernel(a_ref, w1_ref, w2_ref, b1_ref, b2_ref, o_ref):
    a = a_ref[...]
    lin = _dot(a, _wblock(w1_ref)) + b1_ref[...]
    gate = _dot(a, _wblock(w2_ref)) + b2_ref[...]
    o_ref[...] = (lin * _sigmoid(gate)).astype(o_ref.dtype)


def _mm_glu(a, w, layer, b):
    m, k = a.shape
    n = w.shape[2] // 2
    tm = _tile(m, 1024, SUBLANES)
    tn = _tile(n, 256, LANES)
    nj = n // tn
    b2d = b.reshape(1, 2 * n)
    return pl.pallas_call(
        _mm_glu_kernel,
        grid=(m // tm, nj),
        in_specs=[pl.BlockSpec((tm, k), lambda i, j: (i, 0)),
                  _wspec(w, tn, layer, lambda j: j),
                  _wspec(w, tn, layer, lambda j: j + nj),
                  pl.BlockSpec((1, tn), lambda i, j: (0, j)),
                  pl.BlockSpec((1, tn), lambda i, j: (0, j + nj))],
        out_specs=pl.BlockSpec((tm, tn), lambda i, j: (i, j)),
        out_shape=jax.ShapeDtypeStruct((m, n), F32),
        compiler_params=_params("arbitrary", "arbitrary"),
        name="mm_glu",
    )(a, w, w, b2d, b2d)


DOWN_TN = 256


def _mm_swiglu_kernel(a_ref, wg_ref, wu_ref, wd_ref, o_ref, wd_out_ref):
    a = a_ref[...]
    o_ref[...] = (_silu(_dot(a, _wblock(wg_ref))) * _dot(a, _wblock(wu_ref))).astype(o_ref.dtype)
    wd = wd_ref[...]
    tn = wd_out_ref.shape[2]
    for jb in range(wd_out_ref.shape[0]):
        wd_out_ref[jb] = wd[:, jb * tn:(jb + 1) * tn].astype(wd_out_ref.dtype)


def _mm_swiglu(a, w, w_down, layer):
    m, k = a.shape
    n = w.shape[2] // 2
    d = w_down.shape[2]
    tm = _tile(m, 1024, SUBLANES)
    tn = _tile(n, 256, LANES)
    nj = n // tn
    dn = _tile(d, DOWN_TN, LANES)
    slab, rem = divmod(n, (m // tm) * nj)
    assert rem == 0 and slab % (2 * SUBLANES) == 0, "w_down rows must split evenly over the grid steps"
    return pl.pallas_call(
        _mm_swiglu_kernel,
        grid=(m // tm, nj),
        in_specs=[pl.BlockSpec((tm, k), lambda i, j: (i, 0)),
                  _wspec(w, tn, layer, lambda j: j),
                  _wspec(w, tn, layer, lambda j: j + nj),
                  pl.BlockSpec((pl.Squeezed(), slab, d), lambda i, j: (layer, i * nj + j, 0))],
        out_specs=[pl.BlockSpec((tm, tn), lambda i, j: (i, j)),
                   pl.BlockSpec((d // dn, slab, dn), lambda i, j: (0, i * nj + j, 0))],
        out_shape=[jax.ShapeDtypeStruct((m, n), BF16), jax.ShapeDtypeStruct((d // dn, n, dn), BF16)],
        compiler_params=_params("arbitrary", "arbitrary"),
        name="mm_swiglu",
    )(a, w, w, w_down)


def _mm_resid_kernel(a_ref, w_ref, bias_ref, gate_ref, x_ref, o_ref):
    y = _dot(a_ref[...], _wblock(w_ref)) + bias_ref[...]
    o_ref[...] = x_ref[...] + gate_ref[...] * y


def _mm_resid(a, w, layer, bias, gate, x):
    m, k = a.shape
    if layer is None:
        tn = w.shape[2]
        n = w.shape[0] * tn
        w_spec = pl.BlockSpec((pl.Squeezed(), k, tn), lambda i, j: (j, 0, 0))
    else:
        n = w.shape[2]
        tn = _tile(n, 512, LANES)
        w_spec = _wspec(w, tn, layer, lambda j: j)
    tm = _tile(m, 512 if k > 8192 else 1024, SUBLANES)
    vec = pl.BlockSpec((1, tn), lambda i, j: (0, j))
    return pl.pallas_call(
        _mm_resid_kernel,
        grid=(m // tm, n // tn),
        in_specs=[pl.BlockSpec((tm, k), lambda i, j: (i, 0)),
                  w_spec,
                  vec, vec,
                  pl.BlockSpec((tm, tn), lambda i, j: (i, j))],
        out_specs=pl.BlockSpec((tm, tn), lambda i, j: (i, j)),
        out_shape=jax.ShapeDtypeStruct((m, n), F32),
        compiler_params=_params("arbitrary", "arbitrary"),
        name="mm_resid",
    )(a, w, bias.reshape(1, n), gate.reshape(1, n), x)


def _mm_sigmoid_kernel(a_ref, w_ref, b_ref, o_ref):
    o_ref[...] = _sigmoid(_dot(a_ref[...], _wblock(w_ref)) + b_ref[...])


def _mm_sigmoid(a, w, layer, b):
    m, k = a.shape
    n = w.shape[2]
    tm = _tile(m, 1024, SUBLANES)
    return pl.pallas_call(
        _mm_sigmoid_kernel,
        grid=(m // tm, 1),
        in_specs=[pl.BlockSpec((tm, k), lambda i, j: (i, 0)),
                  _wspec(w, n, layer, lambda j: 0),
                  pl.BlockSpec((1, n), lambda i, j: (0, 0))],
        out_specs=pl.BlockSpec((tm, n), lambda i, j: (i, 0)),
        out_shape=jax.ShapeDtypeStruct((m, n), F32),
        compiler_params=_params("arbitrary", "arbitrary"),
        name="mm_sigmoid",
    )(a, w, b.reshape(1, n))


HEADS_CHUNK = 128


def _mm_heads_kernel(a_ref, w_ref, g_ref, cos_ref, sin_ref, o_ref, y_ref, *, rope_every, out_scale):
    j = pl.program_id(1)
    nh, tm, _ = o_ref.shape
    y = _dot(a_ref[...], _wblock(w_ref))

    def plain():
        for h in range(nh):
            o_ref[h] = y[:, h * HEAD_DIM:(h + 1) * HEAD_DIM].astype(o_ref.dtype)

    def roped():
        y_ref[...] = y
        g = g_ref[0]
        chunk = min(HEADS_CHUNK, tm)

        def body(c, carry):
            rs = pl.ds(pl.multiple_of(c * chunk, chunk), chunk)
            cos, sin = cos_ref[rs, :], sin_ref[rs, :]
            for h in range(nh):
                v = _rms_rope(y_ref[rs, h * HEAD_DIM:(h + 1) * HEAD_DIM], g, cos, sin)
                o_ref[h, rs, :] = (v * out_scale).astype(o_ref.dtype)
            return carry

        lax.fori_loop(0, tm // chunk, body, 0, unroll=2)

    if rope_every == 0:
        plain()
    elif rope_every == 1:
        roped()
    else:
        pl.when(j % rope_every == 0)(roped)
        pl.when(j % rope_every != 0)(plain)


def _mm_heads(a, w, layer, col0, n_cols, gains, gain_of_block, cos, sin, *, rope_every, out_scale, out_dtype):
    m, k = a.shape
    tn = 4 * HEAD_DIM
    nh = tn // HEAD_DIM
    tm = _tile(m, 1024, SUBLANES)
    j0 = col0 // tn
    ng = gains.shape[0]
    return pl.pallas_call(
        functools.partial(_mm_heads_kernel, rope_every=rope_every, out_scale=out_scale),
        grid=(m // tm, n_cols // tn),
        in_specs=[pl.BlockSpec((tm, k), lambda i, j: (i, 0)),
                  _wspec(w, tn, layer, lambda j: j + j0),
                  pl.BlockSpec((1, 1, HEAD_DIM), lambda i, j: (gain_of_block(j), 0, 0)),
                  pl.BlockSpec((tm, HEAD_DIM), lambda i, j: (i, 0)),
                  pl.BlockSpec((tm, HEAD_DIM), lambda i, j: (i, 0))],
        out_specs=pl.BlockSpec((nh, tm, HEAD_DIM), lambda i, j: (j, i, 0)),
        out_shape=jax.ShapeDtypeStruct((n_cols // HEAD_DIM, m, HEAD_DIM), out_dtype),
        scratch_shapes=[pltpu.VMEM((tm, tn), F32)],
        compiler_params=_params("arbitrary", "arbitrary"),
        name="mm_heads",
    )(a, w, gains.reshape(ng, 1, HEAD_DIM), cos, sin)


CONV_ROWS = 64
CONV_HALO = 32


def _conv_kernel(halo_ref, u_ref, wdw_ref, bdw_ref, g_ref, b_ref, o_ref, buf_ref, acc_ref):
    i = pl.program_id(0)
    tm, d = u_ref.shape
    cw = wdw_ref.shape[0]
    buf_ref[pl.ds(CONV_HALO, tm), :] = u_ref[...]
    buf_ref[pl.ds(0, CONV_HALO), :] = jnp.where(i > 0, halo_ref[...], 0.0)
    off = CONV_HALO - (cw - 1)

    def col_body(c, carry):
        c0 = pl.multiple_of(c * LANES, LANES)
        acc = jnp.zeros((tm, LANES), F32)
        window = buf_ref[:, pl.ds(c0, LANES)]
        n_buf = window.shape[0]
        for b in range(SUBLANES):
            taps = [(a, SUBLANES * a + b - off) for a in range((off + cw - 1) // SUBLANES + 1)
                    if 0 <= SUBLANES * a + b - off < cw]
            shifted = window if b == 0 else pltpu.roll(window, n_buf - b, axis=0)
            for a, w in taps:
                acc = acc + shifted[SUBLANES * a:SUBLANES * a + tm, :] * wdw_ref[pl.ds(w, 1), pl.ds(c0, LANES)]
        acc_ref[:, pl.ds(c0, LANES)] = acc + bdw_ref[:, pl.ds(c0, LANES)]
        return carry

    lax.fori_loop(0, d // LANES, col_body, 0)
    y = acc_ref[...]
    mu = jnp.mean(y, axis=-1, keepdims=True)
    yc = y - mu
    var = jnp.mean(yc * yc, axis=-1, keepdims=True)
    z = yc * lax.rsqrt(var + EPS) * g_ref[...] + b_ref[...]
    o_ref[...] = _silu(z).astype(o_ref.dtype)


def _conv_ln_silu(u, w_dw, b_dw, ln_g, ln_b):
    s, d = u.shape
    cw = w_dw.shape[0]
    assert cw - 1 <= CONV_HALO
    tm = _tile(s, CONV_ROWS, CONV_HALO)
    ratio = tm // CONV_HALO
    vec = pl.BlockSpec((1, d), lambda i: (0, 0))
    return pl.pallas_call(
        _conv_kernel,
        grid=(s // tm,),
        in_specs=[pl.BlockSpec((CONV_HALO, d), lambda i: (jnp.maximum(i * ratio - 1, 0), 0)),
                  pl.BlockSpec((tm, d), lambda i: (i, 0)),
                  pl.BlockSpec((cw, d), lambda i: (0, 0)),
                  vec, vec, vec],
        out_specs=pl.BlockSpec((tm, d), lambda i: (i, 0)),
        out_shape=jax.ShapeDtypeStruct((s, d), BF16),
        scratch_shapes=[pltpu.VMEM((CONV_HALO + tm, d), F32), pltpu.VMEM((tm, d), F32)],
        compiler_params=_params("arbitrary"),
        name="conv_ln_silu",
    )(u, u, w_dw, b_dw.reshape(1, d), ln_g.reshape(1, d), ln_b.reshape(1, d))


def _compress_kernel(c_ref, p_ref, w1_ref, b1_ref, w2_ref, b2_ref, g_ref, cos_ref, sin_ref, o_ref):
    j = pl.program_id(0)
    nch, half = c_ref.shape
    c = c_ref[...]
    a_lo = (c + p_ref[0:1, :]).astype(BF16)
    a_hi = (c + p_ref[1:2, :]).astype(BF16)
    h_lo = _dot(a_lo, w1_ref[pl.ds(0, half), :])
    h_hi = _dot(a_hi, w1_ref[pl.ds(half, half), :])
    hid = h_lo + pltpu.roll(h_hi, nch - 1, axis=0) + b1_ref[...]
    y = _dot(_silu(hid).astype(BF16), w2_ref[...]) + b2_ref[...]

    @pl.when(j == 0)
    def _():
        o_ref[...] = _rms_rope(y, g_ref[...], cos_ref[...], sin_ref[...]).astype(o_ref.dtype)

    @pl.when(j != 0)
    def _():
        o_ref[...] = y.astype(o_ref.dtype)


def _compress(raw, cmp_pos, w1, b1, w2, b2, g, cos_c, sin_c):
    _, ng, s, dh = raw.shape
    nch = s // CMP_STRIDE
    half = CMP_STRIDE * dh
    hid = w1.shape[-1]
    chunks = raw.reshape(2, ng, nch, half)
    pos2 = cmp_pos.reshape(2, CMP_LEN // CMP_STRIDE, half)
    sq = pl.Squeezed()
    return pl.pallas_call(
        _compress_kernel,
        grid=(2, ng),
        in_specs=[pl.BlockSpec((sq, sq, nch, half), lambda j, gi: (j, gi, 0, 0)),
                  pl.BlockSpec((sq, CMP_LEN // CMP_STRIDE, half), lambda j, gi: (j, 0, 0)),
                  pl.BlockSpec((sq, 2 * half, hid), lambda j, gi: (j, 0, 0)),
                  pl.BlockSpec((sq, 1, hid), lambda j, gi: (j, 0, 0)),
                  pl.BlockSpec((sq, hid, dh), lambda j, gi: (j, 0, 0)),
                  pl.BlockSpec((sq, 1, dh), lambda j, gi: (j, 0, 0)),
                  pl.BlockSpec((1, dh), lambda j, gi: (0, 0)),
                  pl.BlockSpec((nch, dh), lambda j, gi: (0, 0)),
                  pl.BlockSpec((nch, dh), lambda j, gi: (0, 0))],
        out_specs=pl.BlockSpec((sq, sq, nch, dh), lambda j, gi: (j, gi, 0, 0)),
        out_shape=jax.ShapeDtypeStruct((2, ng, nch, dh), BF16),
        compiler_params=_params("arbitrary", "arbitrary"),
        name="compress",
    )(chunks, pos2, w1, b1.reshape(2, 1, hid), w2, b2.reshape(2, 1, dh), g.reshape(1, dh), cos_c, sin_c)


NSA_TQ = 128
NSA_TK = 512


def _nsa_kernel(q_ref, kc_ref, vc_ref, ksa_ref, vsa_ref, kw_ref, vw_ref, gate_ref, mt_ref, o_ref,
                cnt_ref, qa_ref, sa_ref, sb_ref, pc_ref, pw_ref, ps_ref, m_ref, alpha_ref, acc_ref, out_ref):
    r, tq, dh = q_ref.shape
    rows = r * tq
    nc = kc_ref.shape[0]
    ns = mt_ref.shape[0]
    s_len = kw_ref.shape[0]
    n_aug = ksa_ref.shape[1] - dh
    tk = min(NSA_TK, s_len)
    wk = min(WINDOW + tq, s_len)
    qt = pl.program_id(1)
    t0 = qt * tq
    q2 = q_ref[...].reshape(rows, dh)
    t_q = t0 + lax.broadcasted_iota(jnp.int32, (tq, 1), 0)
    gates = gate_ref[...]
    heads = [slice(h * tq, (h + 1) * tq) for h in range(r)]

    k_start = pl.multiple_of(jnp.maximum(jnp.minimum(t0 + tq, s_len) - wk, 0), SUBLANES * 2)
    sa_ref[:, :nc] = _dot_nt(q2, kc_ref[...])
    sb_ref[:, :wk] = _dot_nt(q2, kw_ref[pl.ds(k_start, wk), :])

    cmp_end = lax.broadcasted_iota(jnp.int32, (1, nc), 1) * CMP_STRIDE + (CMP_LEN - 1)
    bias_c = jnp.where(cmp_end <= t_q, 0.0, NEG_INF)
    has_block = t_q >= CMP_LEN - 1
    p_sum = jnp.zeros((tq, nc), F32)
    for rs in heads:
        s = sa_ref[rs, :nc] + bias_c
        e = jnp.exp2(s - jnp.max(s, axis=-1, keepdims=True))
        l = jnp.sum(e, axis=-1, keepdims=True)
        p = e * jnp.where(has_block, 1.0 / l, 0.0)
        p_sum = p_sum + p
        pc_ref[rs, :] = p.astype(BF16)

    back = t_q - (k_start + lax.broadcasted_iota(jnp.int32, (1, wk), 1))
    bias_w = jnp.where((back >= 0) & (back < WINDOW), 0.0, NEG_INF)
    for rs in heads:
        s = sb_ref[rs, :wk] + bias_w
        pw_ref[rs, :] = jnp.exp2(s - jnp.max(s, axis=-1, keepdims=True)).astype(BF16)

    o_c = _dot(pc_ref[...], vc_ref[...])
    o_w = _dot(pw_ref[...], vw_ref[pl.ds(k_start, wk), :])
    for h, rs in enumerate(heads):
        out_ref[rs, :] = (gates[:, h:h + 1] * o_c[rs, :]
                          + gates[:, 2 * r + h:2 * r + h + 1] * (o_w[rs, :dh] / o_w[rs, dh:2 * dh]))

    p_hi = p_sum.astype(BF16)
    rem = p_sum - p_hi.astype(F32)
    p_mid = rem.astype(BF16)
    p_lo = (rem - p_mid.astype(F32)).astype(BF16)
    mt = mt_ref[...]
    imp = _dot_nt(mt, p_hi) + _dot_nt(mt, p_mid) + _dot_nt(mt, p_lo)

    t_row = t0 + lax.broadcasted_iota(jnp.int32, (1, tq), 1)
    blk = lax.broadcasted_iota(jnp.int32, (ns, 1), 0)
    dist = t_row // SLC_BLOCK - blk
    forced = (blk < N_INIT_BLOCKS) | ((dist >= 0) & (dist < N_LOCAL_BLOCKS))
    imp = jnp.where(forced, jnp.inf, jnp.where(dist >= 0, imp, -jnp.inf))

    n_rivals = jnp.minimum((t0 + tq - 1) // SLC_BLOCK + 1, ns)
    cnt_ref[...] = jnp.zeros(cnt_ref.shape, F32)
    for c in range(ns // SUBLANES):
        lo = c * SUBLANES

        @pl.when(lo < n_rivals)
        def _():
            below, own, above = imp[:lo], imp[lo:lo + SUBLANES], imp[lo + SUBLANES:]
            own_idx = lo + lax.broadcasted_iota(jnp.int32, (SUBLANES, 1), 0)
            cnt = cnt_ref[...]
            for i in range(lo, lo + SUBLANES):
                row = imp[i:i + 1, :]
                parts = [jnp.where(row > below, 1.0, 0.0),
                         jnp.where((row > own) | ((row == own) & (i < own_idx)), 1.0, 0.0),
                         jnp.where(row >= above, 1.0, 0.0)]
                cnt = cnt + jnp.concatenate([p for p in parts if p.shape[0]], axis=0)
            cnt_ref[...] = cnt

    chosen = (cnt_ref[...] < float(min(SLC_TOPK, ns))) & (dist >= 0)
    not_sel = jnp.where(chosen, 0.0, 1.0).T.astype(BF16)
    if n_aug > ns:
        not_sel = jnp.concatenate([not_sel, jnp.zeros((tq, n_aug - ns), BF16)], axis=1)
    qa_ref[:, :dh] = q2
    for rs in heads:
        qa_ref[rs, dh:] = not_sel

    m_ref[...] = jnp.full(m_ref.shape, NEG_INF, F32)
    acc_ref[...] = jnp.zeros(acc_ref.shape, F32)
    n_tiles = (t0 + tq + tk - 1) // tk
    last_tile = s_len // tk - 1

    def scores(kt):
        k0 = pl.multiple_of(jnp.minimum(kt, last_tile) * tk, tk)
        return _dot_nt(qa_ref[...], ksa_ref[pl.ds(k0, tk), :])

    def absorb(s_ref, kt):
        kpos = kt * tk + lax.broadcasted_iota(jnp.int32, (1, tk), 1)
        bias = jnp.where(kpos <= t_q, 0.0, NEG_INF)
        for rs in heads:
            s = s_ref[rs, :tk] + bias
            m_old = m_ref[rs, :]
            m_new = jnp.maximum(m_old, jnp.max(s, axis=-1, keepdims=True))
            ps_ref[rs, :] = jnp.exp2(s - _lane_tile(m_new, tk)).astype(BF16)
            alpha_ref[rs, :] = jnp.exp2(m_old - m_new)
            m_ref[rs, :] = m_new
        k0 = pl.multiple_of(jnp.minimum(kt, last_tile) * tk, tk)
        pv = _dot(ps_ref[...], vsa_ref[pl.ds(k0, tk), :])
        acc_ref[...] = acc_ref[...] * _lane_tile(alpha_ref[...], acc_ref.shape[1]) + pv

    sa_ref[:, :tk] = scores(0)

    def pair_body(j, carry):
        kt = 2 * j
        sb_ref[:, :tk] = scores(kt + 1)
        absorb(sa_ref, kt)
        sa_ref[:, :tk] = scores(kt + 2)
        absorb(sb_ref, kt + 1)
        return carry

    lax.fori_loop(0, n_tiles // 2, pair_body, 0)

    @pl.when(n_tiles % 2 == 1)
    def _():
        absorb(sa_ref, n_tiles - 1)

    for h, rs in enumerate(heads):
        o_s = acc_ref[rs, :dh] / acc_ref[rs, dh:2 * dh]
        o_ref[:, h * dh:(h + 1) * dh] = (out_ref[rs, :] + gates[:, r + h:r + h + 1] * o_s).astype(o_ref.dtype)


def _cmp_to_slc_t(n_chunks, n_slc):
    n_cmp = n_chunks - 1
    r, cl = SLC_BLOCK // CMP_STRIDE, CMP_LEN // CMP_STRIDE
    offs = (np.arange(r)[:, None] - np.arange(cl)[None, :]).reshape(-1)
    tgt = r * np.arange(n_slc)[None, :, None] + offs[None, None, :]
    m = (np.arange(n_cmp)[:, None, None] == tgt).sum(-1).astype(np.float32)
    out = np.zeros((n_slc, n_chunks), np.float32)
    out[:, :n_cmp] = m.T
    return out


def _with_block_markers(k_slc):
    ng, s, _ = k_slc.shape
    n_aug = _round_up(s // SLC_BLOCK, LANES)
    own_block = (np.arange(s)[:, None] // SLC_BLOCK) == np.arange(n_aug)[None, :]
    marker = jnp.asarray(np.where(own_block, NEG_INF, 0.0), BF16)
    return jnp.concatenate([k_slc, jnp.broadcast_to(marker, (ng, s, n_aug))], axis=-1)


def _with_ones(v):
    return jnp.concatenate([v, jnp.ones_like(v)], axis=-1)


def _nsa_attention(q, k_cmp, v_cmp, k_aug, v_aug, k_win, v_win, gates):
    n_heads, s, dh = q.shape
    ng = k_win.shape[0]
    r = n_heads // ng
    nc = k_cmp.shape[1]
    ns = s // SLC_BLOCK
    tq = _tile(s, NSA_TQ, SLC_BLOCK)
    rows = r * tq
    tk = min(NSA_TK, s)
    wk = min(WINDOW + tq, s)
    width = max(nc, tk, wk)
    mt = jnp.asarray(_cmp_to_slc_t(nc, ns), BF16)
    sq = pl.Squeezed()
    per_group = lambda arr: pl.BlockSpec((sq,) + arr.shape[1:], lambda gi, qi: (gi, 0, 0))
    return pl.pallas_call(
        _nsa_kernel,
        grid=(ng, s // tq),
        in_specs=[pl.BlockSpec((r, tq, dh), lambda gi, qi: (gi, qi, 0)),
                  per_group(k_cmp), per_group(v_cmp), per_group(k_aug), per_group(v_aug),
                  per_group(k_win), per_group(v_win),
                  pl.BlockSpec((sq, tq, 3 * r), lambda gi, qi: (gi, qi, 0)),
                  pl.BlockSpec((ns, nc), lambda gi, qi: (0, 0))],
        out_specs=pl.BlockSpec((tq, r * dh), lambda gi, qi: (qi, gi)),
        out_shape=jax.ShapeDtypeStruct((s, n_heads * dh), BF16),
        scratch_shapes=[pltpu.VMEM((ns, tq), F32),
                        pltpu.VMEM((rows, k_aug.shape[2]), BF16),
                        pltpu.VMEM((rows, width), F32),
                        pltpu.VMEM((rows, width), F32),
                        pltpu.VMEM((rows, nc), BF16),
                        pltpu.VMEM((rows, wk), BF16),
                        pltpu.VMEM((rows, tk), BF16),
                        pltpu.VMEM((rows, LANES), F32),
                        pltpu.VMEM((rows, LANES), F32),
                        pltpu.VMEM((rows, v_aug.shape[2]), F32),
                        pltpu.VMEM((rows, dh), F32)],
        compiler_params=_params("arbitrary", "arbitrary"),
        name="nsa_attention",
    )(q, k_cmp, v_cmp, k_aug, v_aug, k_win, v_win, gates, mt)


def kernel(x, c, positions, w_ada, b_ada, ada_emb, kv_ada_emb, norm_mix, norm_ffn, norm_kv, conv_w_pw1, conv_b_pw1, conv_w_dw, conv_b_dw, conv_ln_g, conv_ln_b, conv_w_pw2, conv_b_pw2, nsa_w_q, nsa_q_norm, nsa_w_gate, nsa_b_gate, nsa_w_o, kv_w, kv_k_norm, cmp_pos, cmp_w1, cmp_b1, cmp_w2, cmp_b2, ffn_w_gu, ffn_w_down):
    batch, s, d = x.shape
    assert batch == 1, "the kernels are written for a single sequence"
    depth = ada_emb.shape[0]
    n_conv = conv_w_pw1.shape[0]
    ng = N_KV_GROUPS
    n_heads = d // HEAD_DIM
    r = n_heads // ng
    kvw = ng * HEAD_DIM
    zeros_d = jnp.zeros((d,), F32)

    xs = x[0]
    mod = _ada_proj(c, w_ada, b_ada).reshape(6, d)
    pos = positions[0]
    cos, sin = _rope_tables(pos)
    kv = None
    for layer in range(depth):
        m = mod + ada_emb[layer]
        h = _prenorm(xs, norm_mix[layer], m[0], m[1])
        if layer < n_conv:
            u = _mm_glu(h, conv_w_pw1, layer, conv_b_pw1[layer])
            v = _conv_ln_silu(u, conv_w_dw[layer], conv_b_dw[layer], conv_ln_g[layer], conv_ln_b[layer])
            xs = _mm_resid(v, conv_w_pw2, layer, conv_b_pw2[layer], m[2], xs)
        else:
            if kv is None:
                h_kv = _prenorm(xs, norm_kv, mod[0] + kv_ada_emb[0], mod[1] + kv_ada_emb[1])
                kv_w3 = kv_w[None]
                raw = _mm_heads(h_kv, kv_w3, 0, 0, 2 * kvw, kv_k_norm, lambda j: 0, cos, sin,
                                rope_every=0, out_scale=1.0, out_dtype=F32)
                rest = _mm_heads(h_kv, kv_w3, 0, 2 * kvw, 4 * kvw, kv_k_norm, lambda j: 1 + j // 2, cos, sin,
                                 rope_every=2, out_scale=1.0, out_dtype=BF16)
                rest = rest.reshape(4, ng, s, HEAD_DIM)
                n_chunks = s // CMP_STRIDE
                pos_c = jnp.pad(pos[CMP_LEN - 1::CMP_STRIDE], (0, 1))[:n_chunks]
                cos_c, sin_c = _rope_tables(pos_c)
                cmp = _compress(raw.reshape(2, ng, s, HEAD_DIM), cmp_pos, cmp_w1.astype(BF16), cmp_b1,
                                cmp_w2.astype(BF16), cmp_b2, kv_k_norm[0], cos_c, sin_c)
                kv = (cmp[0], cmp[1], _with_block_markers(rest[0]), _with_ones(rest[1]),
                      rest[2], _with_ones(rest[3]))
            i = layer - n_conv
            q = _mm_heads(h, nsa_w_q, i, 0, n_heads * HEAD_DIM, nsa_q_norm[i:i + 1],
                          lambda j: 0, cos, sin, rope_every=1, out_scale=HEAD_DIM ** -0.5 * LOG2_E,
                          out_dtype=BF16)
            gates = _mm_sigmoid(h, nsa_w_gate, i, nsa_b_gate[i])
            gates = gates.reshape(s, 3, ng, r).transpose(2, 0, 1, 3).reshape(ng, s, 3 * r)
            o = _nsa_attention(q, *kv, gates)
            xs = _mm_resid(o, nsa_w_o, i, zeros_d, m[2], xs)
        h = _prenorm(xs, norm_ffn[layer], m[3], m[4])
        a, w_down = _mm_swiglu(h, ffn_w_gu, ffn_w_down, layer)
        xs = _mm_resid(a, w_down, None, zeros_d, m[5], xs)
    return xs[None]
```

```python
import functools
import math

import numpy as np
import jax
import jax.numpy as jnp
from jax import lax
from jax.experimental import pallas as pl
from jax.experimental.pallas import tpu as pltpu

HEAD_DIM = 128
N_KV_GROUPS = 4
CMP_LEN = 32
CMP_STRIDE = 16
SLC_BLOCK = 64
SLC_TOPK = 16
N_INIT_BLOCKS = 1
N_LOCAL_BLOCKS = 2
WINDOW = 512
ROPE_THETA = 10000.0
EPS = 1e-6
NEG_INF = -1e30
LOG2_E = math.log2(math.e)

LANES = 128
SUBLANES = 8
VMEM_LIMIT_BYTES = 56 * 1024 * 1024

BF16 = jnp.bfloat16
F32 = jnp.float32


def _params(*sem):
    return pltpu.CompilerParams(dimension_semantics=sem, vmem_limit_bytes=VMEM_LIMIT_BYTES)


def _tile(n, target, quantum):
    if n <= target:
        return n
    t = (target // quantum) * quantum
    while t >= quantum:
        if n % t == 0:
            return t
        t -= quantum
    raise ValueError(f"no tile for {n} with quantum {quantum}")


def _round_up(n, q):
    return -(-n // q) * q


def _dot(a, b):
    return jnp.dot(a, b, preferred_element_type=F32)


def _dot_nt(a, b):
    return lax.dot_general(a, b, (((1,), (1,)), ((), ())), preferred_element_type=F32)


def _sigmoid(v):
    return 1.0 / (1.0 + jnp.exp(-v))


def _silu(v):
    return v * _sigmoid(v)


def _lane_tile(v, width):
    reps = width // v.shape[1]
    return v if reps == 1 else jnp.concatenate([v] * reps, axis=1)


def _ada_kernel(cb_ref, w_ref, b_ref, o_ref):
    d, tn = w_ref.shape
    n_chunks = tn // LANES

    def body(k, accs):
        r = pl.multiple_of(k * SUBLANES, SUBLANES)
        cv = _silu(cb_ref[pl.ds(r, SUBLANES), :])
        return tuple(
            accs[j] + cv * w_ref[pl.ds(r, SUBLANES), j * LANES:(j + 1) * LANES]
            for j in range(n_chunks))

    accs = lax.fori_loop(0, d // SUBLANES, body,
                         tuple(jnp.zeros((SUBLANES, LANES), F32) for _ in range(n_chunks)), unroll=4)
    for j in range(n_chunks):
        o_ref[:, j * LANES:(j + 1) * LANES] = (
            jnp.sum(accs[j], axis=0, keepdims=True) + b_ref[:, j * LANES:(j + 1) * LANES])


def _ada_proj(c, w_ada, b_ada):
    d, n = w_ada.shape
    tn = _tile(n, 1024, LANES)
    cb = jnp.broadcast_to(c.reshape(d, 1), (d, LANES))
    return pl.pallas_call(
        _ada_kernel,
        grid=(n // tn,),
        in_specs=[pl.BlockSpec((d, LANES), lambda j: (0, 0)),
                  pl.BlockSpec((d, tn), lambda j: (0, j)),
                  pl.BlockSpec((1, tn), lambda j: (0, j))],
        out_specs=pl.BlockSpec((1, tn), lambda j: (0, j)),
        out_shape=jax.ShapeDtypeStruct((1, n), F32),
        compiler_params=_params("arbitrary"),
        name="ada_proj",
    )(cb, w_ada, b_ada.reshape(1, n))


def _rope_table_kernel(pos_ref, inv_ref, cos_ref, sin_ref):
    ang = pos_ref[...].astype(F32) * inv_ref[...]
    lane = lax.broadcasted_iota(jnp.int32, ang.shape, 1)
    cos_ref[...] = jnp.cos(ang)
    s = jnp.sin(ang)
    sin_ref[...] = jnp.where(lane < HEAD_DIM // 2, -s, s)


def _rope_tables(pos):
    n = pos.shape[0]
    half = HEAD_DIM // 2
    inv = ROPE_THETA ** (-jnp.arange(half, dtype=F32) / half)
    inv2 = jnp.concatenate([inv, inv]).reshape(1, HEAD_DIM)
    tm = _tile(n, 1024, SUBLANES)
    return pl.pallas_call(
        _rope_table_kernel,
        grid=(n // tm,),
        in_specs=[pl.BlockSpec((tm, 1), lambda i: (i, 0)),
                  pl.BlockSpec((1, HEAD_DIM), lambda i: (0, 0))],
        out_specs=[pl.BlockSpec((tm, HEAD_DIM), lambda i: (i, 0)),
                   pl.BlockSpec((tm, HEAD_DIM), lambda i: (i, 0))],
        out_shape=[jax.ShapeDtypeStruct((n, HEAD_DIM), F32)] * 2,
        compiler_params=_params("arbitrary"),
        name="rope_tables",
    )(pos.reshape(n, 1), inv2)


def _rms_rope(v, g, cos, sin):
    vg = v * g
    rotated = vg * cos + pltpu.roll(vg, HEAD_DIM // 2, axis=1) * sin
    sq = v * v
    hi = sq.astype(BF16)
    lo = (sq - hi.astype(F32)).astype(BF16)
    ones = jnp.ones((HEAD_DIM, HEAD_DIM), BF16)
    sum_sq = _dot(hi, ones) + _dot(lo, ones)
    return rotated * lax.rsqrt(sum_sq * (1.0 / HEAD_DIM) + EPS)


def _prenorm_kernel(x_ref, g_ref, shift_ref, scale_ref, o_ref):
    x = x_ref[...]
    y = x * lax.rsqrt(jnp.mean(x * x, axis=-1, keepdims=True) + EPS) * g_ref[...]
    o_ref[...] = (y * (1.0 + scale_ref[...]) + shift_ref[...]).astype(o_ref.dtype)


def _prenorm(x, g, shift, scale):
    s, d = x.shape
    tm = _tile(s, 512, SUBLANES)
    vec = pl.BlockSpec((1, d), lambda i: (0, 0))
    return pl.pallas_call(
        _prenorm_kernel,
        grid=(s // tm,),
        in_specs=[pl.BlockSpec((tm, d), lambda i: (i, 0)), vec, vec, vec],
        out_specs=pl.BlockSpec((tm, d), lambda i: (i, 0)),
        out_shape=jax.ShapeDtypeStruct((s, d), BF16),
        compiler_params=_params("arbitrary"),
        name="prenorm",
    )(x, g.reshape(1, d), shift.reshape(1, d), scale.reshape(1, d))


def _wblock(w_ref):
    w = w_ref[...]
    return w if w.dtype == BF16 else w.astype(BF16)


def _wspec(w, tn, layer, col_block):
    return pl.BlockSpec((pl.Squeezed(), w.shape[1], tn), lambda i, j: (layer, 0, col_block(j)))


def _mm_glu_kernel(a_ref, w1_ref, w2_ref, b1_ref, b2_ref, o_ref):
    a = a_ref[...]
    lin = _dot(a, _wblock(w1_ref)) + b1_ref[...]
    gate = _dot(a, _wblock(w2_ref)) + b2_ref[...]
    o_ref[...] = (lin * _sigmoid(gate)).astype(o_ref.dtype)


def _mm_glu(a, w, layer, b):
    m, k = a.shape
    n = w.shape[2] // 2
    tm = _tile(m, 1024, SUBLANES)
    tn = _tile(n, 256, LANES)
    nj = n // tn
    b2d = b.reshape(1, 2 * n)
    return pl.pallas_call(
        _mm_glu_kernel,
        grid=(m // tm, nj),
        in_specs=[pl.BlockSpec((tm, k), lambda i, j: (i, 0)),
                  _wspec(w, tn, layer, lambda j: j),
                  _wspec(w, tn, layer, lambda j: j + nj),
                  pl.BlockSpec((1, tn), lambda i, j: (0, j)),
                  pl.BlockSpec((1, tn), lambda i, j: (0, j + nj))],
        out_specs=pl.BlockSpec((tm, tn), lambda i, j: (i, j)),
        out_shape=jax.ShapeDtypeStruct((m, n), F32),
        compiler_params=_params("arbitrary", "arbitrary"),
        name="mm_glu",
    )(a, w, w, b2d, b2d)


DOWN_TN = 256


def _mm_swiglu_kernel(a_ref, wg_ref, wu_ref, wd_ref, o_ref, wd_out_ref):
    a = a_ref[...]
    o_ref[...] = (_silu(_dot(a, _wblock(wg_ref))) * _dot(a, _wblock(wu_ref))).astype(o_ref.dtype)
    wd = wd_ref[...]
    tn = wd_out_ref.shape[2]
    for jb in range(wd_out_ref.shape[0]):
        wd_out_ref[jb] = wd[:, jb * tn:(jb + 1) * tn].astype(wd_out_ref.dtype)


def _mm_swiglu(a, w, w_down, layer):
    m, k = a.shape
    n = w.shape[2] // 2
    d = w_down.shape[2]
    tm = _tile(m, 1024, SUBLANES)
    tn = _tile(n, 256, LANES)
    nj = n // tn
    dn = _tile(d, DOWN_TN, LANES)
    slab, rem = divmod(n, (m // tm) * nj)
    assert rem == 0 and slab % (2 * SUBLANES) == 0, "w_down rows must split evenly over the grid steps"
    return pl.pallas_call(
        _mm_swiglu_kernel,
        grid=(m // tm, nj),
        in_specs=[pl.BlockSpec((tm, k), lambda i, j: (i, 0)),
                  _wspec(w, tn, layer, lambda j: j),
                  _wspec(w, tn, layer, lambda j: j + nj),
                  pl.BlockSpec((pl.Squeezed(), slab, d), lambda i, j: (layer, i * nj + j, 0))],
        out_specs=[pl.BlockSpec((tm, tn), lambda i, j: (i, j)),
                   pl.BlockSpec((d // dn, slab, dn), lambda i, j: (0, i * nj + j, 0))],
        out_shape=[jax.ShapeDtypeStruct((m, n), BF16), jax.ShapeDtypeStruct((d // dn, n, dn), BF16)],
        compiler_params=_params("arbitrary", "arbitrary"),
        name="mm_swiglu",
    )(a, w, w, w_down)


def _mm_resid_kernel(a_ref, w_ref, bias_ref, gate_ref, x_ref, o_ref):
    y = _dot(a_ref[...], _wblock(w_ref)) + bias_ref[...]
    o_ref[...] = x_ref[...] + gate_ref[...] * y


def _mm_resid(a, w, layer, bias, gate, x):
    m, k = a.shape
    if layer is None:
        tn = w.shape[2]
        n = w.shape[0] * tn
        w_spec = pl.BlockSpec((pl.Squeezed(), k, tn), lambda i, j: (j, 0, 0))
    else:
        n = w.shape[2]
        tn = _tile(n, 512, LANES)
        w_spec = _wspec(w, tn, layer, lambda j: j)
    tm = _tile(m, 512 if k > 8192 else 1024, SUBLANES)
    vec = pl.BlockSpec((1, tn), lambda i, j: (0, j))
    return pl.pallas_call(
        _mm_resid_kernel,
        grid=(m // tm, n // tn),
        in_specs=[pl.BlockSpec((tm, k), lambda i, j: (i, 0)),
                  w_spec,
                  vec, vec,
                  pl.BlockSpec((tm, tn), lambda i, j: (i, j))],
        out_specs=pl.BlockSpec((tm, tn), lambda i, j: (i, j)),
        out_shape=jax.ShapeDtypeStruct((m, n), F32),
        compiler_params=_params("arbitrary", "arbitrary"),
        name="mm_resid",
    )(a, w, bias.reshape(1, n), gate.reshape(1, n), x)


def _mm_sigmoid_kernel(a_ref, w_ref, b_ref, o_ref):
    o_ref[...] = _sigmoid(_dot(a_ref[...], _wblock(w_ref)) + b_ref[...])


def _mm_sigmoid(a, w, layer, b):
    m, k = a.shape
    n = w.shape[2]
    tm = _tile(m, 1024, SUBLANES)
    return pl.pallas_call(
        _mm_sigmoid_kernel,
        grid=(m // tm, 1),
        in_specs=[pl.BlockSpec((tm, k), lambda i, j: (i, 0)),
                  _wspec(w, n, layer, lambda j: 0),
                  pl.BlockSpec((1, n), lambda i, j: (0, 0))],
        out_specs=pl.BlockSpec((tm, n), lambda i, j: (i, 0)),
        out_shape=jax.ShapeDtypeStruct((m, n), F32),
        compiler_params=_params("arbitrary", "arbitrary"),
        name="mm_sigmoid",
    )(a, w, b.reshape(1, n))


HEADS_CHUNK = 128


def _mm_heads_kernel(a_ref, w_ref, g_ref, cos_ref, sin_ref, o_ref, y_ref, *, rope_every, out_scale):
    j = pl.program_id(1)
    nh, tm, _ = o_ref.shape
    y = _dot(a_ref[...], _wblock(w_ref))

    def plain():
        for h in range(nh):
            o_ref[h] = y[:, h * HEAD_DIM:(h + 1) * HEAD_DIM].astype(o_ref.dtype)

    def roped():
        y_ref[...] = y
        g = g_ref[0]
        chunk = min(HEADS_CHUNK, tm)

        def body(c, carry):
            rs = pl.ds(pl.multiple_of(c * chunk, chunk), chunk)
            cos, sin = cos_ref[rs, :], sin_ref[rs, :]
            for h in range(nh):
                v = _rms_rope(y_ref[rs, h * HEAD_DIM:(h + 1) * HEAD_DIM], g, cos, sin)
                o_ref[h, rs, :] = (v * out_scale).astype(o_ref.dtype)
            return carry

        lax.fori_loop(0, tm // chunk, body, 0, unroll=2)

    if rope_every == 0:
        plain()
    elif rope_every == 1:
        roped()
    else:
        pl.when(j % rope_every == 0)(roped)
        pl.when(j % rope_every != 0)(plain)


def _mm_heads(a, w, layer, col0, n_cols, gains, gain_of_block, cos, sin, *, rope_every, out_scale, out_dtype):
    m, k = a.shape
    tn = 4 * HEAD_DIM
    nh = tn // HEAD_DIM
    tm = _tile(m, 1024, SUBLANES)
    j0 = col0 // tn
    ng = gains.shape[0]
    return pl.pallas_call(
        functools.partial(_mm_heads_kernel, rope_every=rope_every, out_scale=out_scale),
        grid=(m // tm, n_cols // tn),
        in_specs=[pl.BlockSpec((tm, k), lambda i, j: (i, 0)),
                  _wspec(w, tn, layer, lambda j: j + j0),
                  pl.BlockSpec((1, 1, HEAD_DIM), lambda i, j: (gain_of_block(j), 0, 0)),
                  pl.BlockSpec((tm, HEAD_DIM), lambda i, j: (i, 0)),
                  pl.BlockSpec((tm, HEAD_DIM), lambda i, j: (i, 0))],
        out_specs=pl.BlockSpec((nh, tm, HEAD_DIM), lambda i, j: (j, i, 0)),
        out_shape=jax.ShapeDtypeStruct((n_cols // HEAD_DIM, m, HEAD_DIM), out_dtype),
        scratch_shapes=[pltpu.VMEM((tm, tn), F32)],
        compiler_params=_params("arbitrary", "arbitrary"),
        name="mm_heads",
    )(a, w, gains.reshape(ng, 1, HEAD_DIM), cos, sin)


CONV_ROWS = 64
CONV_HALO = 32


def _conv_kernel(halo_ref, u_ref, wdw_ref, bdw_ref, g_ref, b_ref, o_ref, buf_ref, acc_ref):
    i = pl.program_id(0)
    tm, d = u_ref.shape
    cw = wdw_ref.shape[0]
    buf_ref[pl.ds(CONV_HALO, tm), :] = u_ref[...]
    buf_ref[pl.ds(0, CONV_HALO), :] = jnp.where(i > 0, halo_ref[...], 0.0)
    off = CONV_HALO - (cw - 1)

    def col_body(c, carry):
        c0 = pl.multiple_of(c * LANES, LANES)
        acc = jnp.zeros((tm, LANES), F32)
        window = buf_ref[:, pl.ds(c0, LANES)]
        n_buf = window.shape[0]
        for b in range(SUBLANES):
            taps = [(a, SUBLANES * a + b - off) for a in range((off + cw - 1) // SUBLANES + 1)
                    if 0 <= SUBLANES * a + b - off < cw]
            shifted = window if b == 0 else pltpu.roll(window, n_buf - b, axis=0)
            for a, w in taps:
                acc = acc + shifted[SUBLANES * a:SUBLANES * a + tm, :] * wdw_ref[pl.ds(w, 1), pl.ds(c0, LANES)]
        acc_ref[:, pl.ds(c0, LANES)] = acc + bdw_ref[:, pl.ds(c0, LANES)]
        return carry

    lax.fori_loop(0, d // LANES, col_body, 0)
    y = acc_ref[...]
    mu = jnp.mean(y, axis=-1, keepdims=True)
    yc = y - mu
    var = jnp.mean(yc * yc, axis=-1, keepdims=True)
    z = yc * lax.rsqrt(var + EPS) * g_ref[...] + b_ref[...]
    o_ref[...] = _silu(z).astype(o_ref.dtype)


def _conv_ln_silu(u, w_dw, b_dw, ln_g, ln_b):
    s, d = u.shape
    cw = w_dw.shape[0]
    assert cw - 1 <= CONV_HALO
    tm = _tile(s, CONV_ROWS, CONV_HALO)
    ratio = tm // CONV_HALO
    vec = pl.BlockSpec((1, d), lambda i: (0, 0))
    return pl.pallas_call(
        _conv_kernel,
        grid=(s // tm,),
        in_specs=[pl.BlockSpec((CONV_HALO, d), lambda i: (jnp.maximum(i * ratio - 1, 0), 0)),
                  pl.BlockSpec((tm, d), lambda i: (i, 0)),
                  pl.BlockSpec((cw, d), lambda i: (0, 0)),
                  vec, vec, vec],
        out_specs=pl.BlockSpec((tm, d), lambda i: (i, 0)),
        out_shape=jax.ShapeDtypeStruct((s, d), BF16),
        scratch_shapes=[pltpu.VMEM((CONV_HALO + tm, d), F32), pltpu.VMEM((tm, d), F32)],
        compiler_params=_params("arbitrary"),
        name="conv_ln_silu",
    )(u, u, w_dw, b_dw.reshape(1, d), ln_g.reshape(1, d), ln_b.reshape(1, d))


def _compress_kernel(c_ref, p_ref, w1_ref, b1_ref, w2_ref, b2_ref, g_ref, cos_ref, sin_ref, o_ref):
    j = pl.program_id(0)
    nch, half = c_ref.shape
    c = c_ref[...]
    a_lo = (c + p_ref[0:1, :]).astype(BF16)
    a_hi = (c + p_ref[1:2, :]).astype(BF16)
    h_lo = _dot(a_lo, w1_ref[pl.ds(0, half), :])
    h_hi = _dot(a_hi, w1_ref[pl.ds(half, half), :])
    hid = h_lo + pltpu.roll(h_hi, nch - 1, axis=0) + b1_ref[...]
    y = _dot(_silu(hid).astype(BF16), w2_ref[...]) + b2_ref[...]

    @pl.when(j == 0)
    def _():
        o_ref[...] = _rms_rope(y, g_ref[...], cos_ref[...], sin_ref[...]).astype(o_ref.dtype)

    @pl.when(j != 0)
    def _():
        o_ref[...] = y.astype(o_ref.dtype)


def _compress(raw, cmp_pos, w1, b1, w2, b2, g, cos_c, sin_c):
    _, ng, s, dh = raw.shape
    nch = s // CMP_STRIDE
    half = CMP_STRIDE * dh
    hid = w1.shape[-1]
    chunks = raw.reshape(2, ng, nch, half)
    pos2 = cmp_pos.reshape(2, CMP_LEN // CMP_STRIDE, half)
    sq = pl.Squeezed()
    return pl.pallas_call(
        _compress_kernel,
        grid=(2, ng),
        in_specs=[pl.BlockSpec((sq, sq, nch, half), lambda j, gi: (j, gi, 0, 0)),
                  pl.BlockSpec((sq, CMP_LEN // CMP_STRIDE, half), lambda j, gi: (j, 0, 0)),
                  pl.BlockSpec((sq, 2 * half, hid), lambda j, gi: (j, 0, 0)),
                  pl.BlockSpec((sq, 1, hid), lambda j, gi: (j, 0, 0)),
                  pl.BlockSpec((sq, hid, dh), lambda j, gi: (j, 0, 0)),
                  pl.BlockSpec((sq, 1, dh), lambda j, gi: (j, 0, 0)),
                  pl.BlockSpec((1, dh), lambda j, gi: (0, 0)),
                  pl.BlockSpec((nch, dh), lambda j, gi: (0, 0)),
                  pl.BlockSpec((nch, dh), lambda j, gi: (0, 0))],
        out_specs=pl.BlockSpec((sq, sq, nch, dh), lambda j, gi: (j, gi, 0, 0)),
        out_shape=jax.ShapeDtypeStruct((2, ng, nch, dh), BF16),
        compiler_params=_params("arbitrary", "arbitrary"),
        name="compress",
    )(chunks, pos2, w1, b1.reshape(2, 1, hid), w2, b2.reshape(2, 1, dh), g.reshape(1, dh), cos_c, sin_c)


NSA_TQ = 128
NSA_TK = 512


def _nsa_kernel(q_ref, kc_ref, vc_ref, ksa_ref, vsa_ref, kw_ref, vw_ref, gate_ref, mt_ref, o_ref,
                cnt_ref, qa_ref, sa_ref, sb_ref, mxa_ref, mxb_ref, pc_ref, pw_ref, m_ref, acc_ref, out_ref):
    r, tq, dh = q_ref.shape
    rows = r * tq
    nc = kc_ref.shape[0]
    ns = mt_ref.shape[0]
    s_len = kw_ref.shape[0]
    n_aug = ksa_ref.shape[1] - dh
    tk = min(NSA_TK, s_len)
    wk = min(WINDOW + tq, s_len)
    qt = pl.program_id(1)
    t0 = qt * tq
    q2 = q_ref[...].reshape(rows, dh)
    t_q = t0 + lax.broadcasted_iota(jnp.int32, (tq, 1), 0)
    gates = gate_ref[...]
    heads = [slice(h * tq, (h + 1) * tq) for h in range(r)]

    k_start = pl.multiple_of(jnp.maximum(jnp.minimum(t0 + tq, s_len) - wk, 0), SUBLANES * 2)
    sa_ref[:, :nc] = _dot_nt(q2, kc_ref[...])
    sb_ref[:, :wk] = _dot_nt(q2, kw_ref[pl.ds(k_start, wk), :])

    cmp_end = lax.broadcasted_iota(jnp.int32, (1, nc), 1) * CMP_STRIDE + (CMP_LEN - 1)
    bias_c = jnp.where(cmp_end <= t_q, 0.0, NEG_INF)
    has_block = t_q >= CMP_LEN - 1
    p_sum = jnp.zeros((tq, nc), F32)
    for rs in heads:
        s = sa_ref[rs, :nc] + bias_c
        e = jnp.exp2(s - jnp.max(s, axis=-1, keepdims=True))
        l = jnp.sum(e, axis=-1, keepdims=True)
        p = e * jnp.where(has_block, 1.0 / l, 0.0)
        p_sum = p_sum + p
        pc_ref[rs, :] = p.astype(BF16)

    back = t_q - (k_start + lax.broadcasted_iota(jnp.int32, (1, wk), 1))
    bias_w = jnp.where((back >= 0) & (back < WINDOW), 0.0, NEG_INF)
    for rs in heads:
        s = sb_ref[rs, :wk] + bias_w
        pw_ref[rs, :] = jnp.exp2(s - jnp.max(s, axis=-1, keepdims=True)).astype(BF16)

    o_c = _dot(pc_ref[...], vc_ref[...])
    o_w = _dot(pw_ref[...], vw_ref[pl.ds(k_start, wk), :])
    for h, rs in enumerate(heads):
        out_ref[rs, :] = (gates[:, h:h + 1] * o_c[rs, :]
                          + gates[:, 2 * r + h:2 * r + h + 1] * (o_w[rs, :dh] / o_w[rs, dh:2 * dh]))

    p_hi = p_sum.astype(BF16)
    rem = p_sum - p_hi.astype(F32)
    p_mid = rem.astype(BF16)
    p_lo = (rem - p_mid.astype(F32)).astype(BF16)
    mt = mt_ref[...]
    imp = _dot_nt(mt, p_hi) + _dot_nt(mt, p_mid) + _dot_nt(mt, p_lo)

    t_row = t0 + lax.broadcasted_iota(jnp.int32, (1, tq), 1)
    blk = lax.broadcasted_iota(jnp.int32, (ns, 1), 0)
    dist = t_row // SLC_BLOCK - blk
    forced = (blk < N_INIT_BLOCKS) | ((dist >= 0) & (dist < N_LOCAL_BLOCKS))
    imp = jnp.where(forced, jnp.inf, jnp.where(dist >= 0, imp, -jnp.inf))

    n_rivals = jnp.minimum((t0 + tq - 1) // SLC_BLOCK + 1, ns)
    cnt_ref[...] = jnp.zeros(cnt_ref.shape, F32)
    for c in range(ns // SUBLANES):
        lo = c * SUBLANES

        @pl.when(lo < n_rivals)
        def _():
            below, own, above = imp[:lo], imp[lo:lo + SUBLANES], imp[lo + SUBLANES:]
            own_idx = lo + lax.broadcasted_iota(jnp.int32, (SUBLANES, 1), 0)
            cnt = cnt_ref[...]
            for i in range(lo, lo + SUBLANES):
                row = imp[i:i + 1, :]
                parts = [jnp.where(row > below, 1.0, 0.0),
                         jnp.where((row > own) | ((row == own) & (i < own_idx)), 1.0, 0.0),
                         jnp.where(row >= above, 1.0, 0.0)]
                cnt = cnt + jnp.concatenate([p for p in parts if p.shape[0]], axis=0)
            cnt_ref[...] = cnt

    chosen = (cnt_ref[...] < float(min(SLC_TOPK, ns))) & (dist >= 0)
    not_sel = jnp.where(chosen, 0.0, 1.0).T.astype(BF16)
    if n_aug > ns:
        not_sel = jnp.concatenate([not_sel, jnp.zeros((tq, n_aug - ns), BF16)], axis=1)
    qa_ref[:, :dh] = q2
    for rs in heads:
        qa_ref[rs, dh:] = not_sel

    m_ref[...] = jnp.full(m_ref.shape, NEG_INF, F32)
    acc_ref[...] = jnp.zeros(acc_ref.shape, F32)
    n_tiles = (t0 + tq + tk - 1) // tk
    last_tile = s_len // tk - 1

    def scores(kt, s_ref, mx_ref):
        k0 = pl.multiple_of(jnp.minimum(kt, last_tile) * tk, tk)
        kpos = kt * tk + lax.broadcasted_iota(jnp.int32, (1, tk), 1)
        bias = jnp.where(kpos <= t_q, 0.0, NEG_INF)
        raw = _dot_nt(qa_ref[...], ksa_ref[pl.ds(k0, tk), :])
        for rs in heads:
            s = raw[rs, :] + bias
            s_ref[rs, :tk] = s
            mx_ref[rs, :] = jnp.broadcast_to(jnp.max(s, axis=-1, keepdims=True), (tq, LANES))

    def absorb(kt, s_ref, mx_ref):
        probs, alphas = [], []
        for rs in heads:
            m_old = m_ref[rs, :]
            m_new = jnp.maximum(m_old, mx_ref[rs, :])
            probs.append(jnp.exp2(s_ref[rs, :tk] - _lane_tile(m_new, tk)).astype(BF16))
            alphas.append(jnp.exp2(m_old - m_new))
            m_ref[rs, :] = m_new
        k0 = pl.multiple_of(jnp.minimum(kt, last_tile) * tk, tk)
        pv = _dot(jnp.concatenate(probs, axis=0), vsa_ref[pl.ds(k0, tk), :])
        alpha = jnp.concatenate(alphas, axis=0)
        acc_ref[...] = acc_ref[...] * _lane_tile(alpha, acc_ref.shape[1]) + pv

    scores(0, sa_ref, mxa_ref)

    def pair_body(j, carry):
        kt = 2 * j
        scores(kt + 1, sb_ref, mxb_ref)
        absorb(kt, sa_ref, mxa_ref)
        scores(kt + 2, sa_ref, mxa_ref)
        absorb(kt + 1, sb_ref, mxb_ref)
        return carry

    lax.fori_loop(0, n_tiles // 2, pair_body, 0)

    @pl.when(n_tiles % 2 == 1)
    def _():
        absorb(n_tiles - 1, sa_ref, mxa_ref)

    for h, rs in enumerate(heads):
        o_s = acc_ref[rs, :dh] / acc_ref[rs, dh:2 * dh]
        o_ref[:, h * dh:(h + 1) * dh] = (out_ref[rs, :] + gates[:, r + h:r + h + 1] * o_s).astype(o_ref.dtype)


def _cmp_to_slc_t(n_chunks, n_slc):
    n_cmp = n_chunks - 1
    r, cl = SLC_BLOCK // CMP_STRIDE, CMP_LEN // CMP_STRIDE
    offs = (np.arange(r)[:, None] - np.arange(cl)[None, :]).reshape(-1)
    tgt = r * np.arange(n_slc)[None, :, None] + offs[None, None, :]
    m = (np.arange(n_cmp)[:, None, None] == tgt).sum(-1).astype(np.float32)
    out = np.zeros((n_slc, n_chunks), np.float32)
    out[:, :n_cmp] = m.T
    return out


def _with_block_markers(k_slc):
    ng, s, _ = k_slc.shape
    n_aug = _round_up(s // SLC_BLOCK, LANES)
    own_block = (np.arange(s)[:, None] // SLC_BLOCK) == np.arange(n_aug)[None, :]
    marker = jnp.asarray(np.where(own_block, NEG_INF, 0.0), BF16)
    return jnp.concatenate([k_slc, jnp.broadcast_to(marker, (ng, s, n_aug))], axis=-1)


def _with_ones(v):
    return jnp.concatenate([v, jnp.ones_like(v)], axis=-1)


def _nsa_attention(q, k_cmp, v_cmp, k_aug, v_aug, k_win, v_win, gates):
    n_heads, s, dh = q.shape
    ng = k_win.shape[0]
    r = n_heads // ng
    nc = k_cmp.shape[1]
    ns = s // SLC_BLOCK
    tq = _tile(s, NSA_TQ, SLC_BLOCK)
    rows = r * tq
    tk = min(NSA_TK, s)
    wk = min(WINDOW + tq, s)
    width = max(nc, tk, wk)
    mt = jnp.asarray(_cmp_to_slc_t(nc, ns), BF16)
    sq = pl.Squeezed()
    per_group = lambda arr: pl.BlockSpec((sq,) + arr.shape[1:], lambda gi, qi: (gi, 0, 0))
    return pl.pallas_call(
        _nsa_kernel,
        grid=(ng, s // tq),
        in_specs=[pl.BlockSpec((r, tq, dh), lambda gi, qi: (gi, qi, 0)),
                  per_group(k_cmp), per_group(v_cmp), per_group(k_aug), per_group(v_aug),
                  per_group(k_win), per_group(v_win),
                  pl.BlockSpec((sq, tq, 3 * r), lambda gi, qi: (gi, qi, 0)),
                  pl.BlockSpec((ns, nc), lambda gi, qi: (0, 0))],
        out_specs=pl.BlockSpec((tq, r * dh), lambda gi, qi: (qi, gi)),
        out_shape=jax.ShapeDtypeStruct((s, n_heads * dh), BF16),
        scratch_shapes=[pltpu.VMEM((ns, tq), F32),
                        pltpu.VMEM((rows, k_aug.shape[2]), BF16),
                        pltpu.VMEM((rows, width), F32),
                        pltpu.VMEM((rows, width), F32),
                        pltpu.VMEM((rows, LANES), F32),
                        pltpu.VMEM((rows, LANES), F32),
                        pltpu.VMEM((rows, nc), BF16),
                        pltpu.VMEM((rows, wk), BF16),
                        pltpu.VMEM((rows, LANES), F32),
                        pltpu.VMEM((rows, v_aug.shape[2]), F32),
                        pltpu.VMEM((rows, dh), F32)],
        compiler_params=_params("arbitrary", "arbitrary"),
        name="nsa_attention",
    )(q, k_cmp, v_cmp, k_aug, v_aug, k_win, v_win, gates, mt)


def kernel(x, c, positions, w_ada, b_ada, ada_emb, kv_ada_emb, norm_mix, norm_ffn, norm_kv, conv_w_pw1, conv_b_pw1, conv_w_dw, conv_b_dw, conv_ln_g, conv_ln_b, conv_w_pw2, conv_b_pw2, nsa_w_q, nsa_q_norm, nsa_w_gate, nsa_b_gate, nsa_w_o, kv_w, kv_k_norm, cmp_pos, cmp_w1, cmp_b1, cmp_w2, cmp_b2, ffn_w_gu, ffn_w_down):
    batch, s, d = x.shape
    assert batch == 1, "the kernels are written for a single sequence"
    depth = ada_emb.shape[0]
    n_conv = conv_w_pw1.shape[0]
    ng = N_KV_GROUPS
    n_heads = d // HEAD_DIM
    r = n_heads // ng
    kvw = ng * HEAD_DIM
    zeros_d = jnp.zeros((d,), F32)

    xs = x[0]
    mod = _ada_proj(c, w_ada, b_ada).reshape(6, d)
    pos = positions[0]
    cos, sin = _rope_tables(pos)
    kv = None
    for layer in range(depth):
        m = mod + ada_emb[layer]
        h = _prenorm(xs, norm_mix[layer], m[0], m[1])
        if layer < n_conv:
            u = _mm_glu(h, conv_w_pw1, layer, conv_b_pw1[layer])
            v = _conv_ln_silu(u, conv_w_dw[layer], conv_b_dw[layer], conv_ln_g[layer], conv_ln_b[layer])
            xs = _mm_resid(v, conv_w_pw2, layer, conv_b_pw2[layer], m[2], xs)
        else:
            if kv is None:
                h_kv = _prenorm(xs, norm_kv, mod[0] + kv_ada_emb[0], mod[1] + kv_ada_emb[1])
                kv_w3 = kv_w[None]
                raw = _mm_heads(h_kv, kv_w3, 0, 0, 2 * kvw, kv_k_norm, lambda j: 0, cos, sin,
                                rope_every=0, out_scale=1.0, out_dtype=F32)
                rest = _mm_heads(h_kv, kv_w3, 0, 2 * kvw, 4 * kvw, kv_k_norm, lambda j: 1 + j // 2, cos, sin,
                                 rope_every=2, out_scale=1.0, out_dtype=BF16)
                rest = rest.reshape(4, ng, s, HEAD_DIM)
                n_chunks = s // CMP_STRIDE
                pos_c = jnp.pad(pos[CMP_LEN - 1::CMP_STRIDE], (0, 1))[:n_chunks]
                cos_c, sin_c = _rope_tables(pos_c)
                cmp = _compress(raw.reshape(2, ng, s, HEAD_DIM), cmp_pos, cmp_w1.astype(BF16), cmp_b1,
                                cmp_w2.astype(BF16), cmp_b2, kv_k_norm[0], cos_c, sin_c)
                kv = (cmp[0], cmp[1], _with_block_markers(rest[0]), _with_ones(rest[1]),
                      rest[2], _with_ones(rest[3]))
            i = layer - n_conv
            q = _mm_heads(h, nsa_w_q, i, 0, n_heads * HEAD_DIM, nsa_q_norm[i:i + 1],
                          lambda j: 0, cos, sin, rope_every=1, out_scale=HEAD_DIM ** -0.5 * LOG2_E,
                          out_dtype=BF16)
            gates = _mm_sigmoid(h, nsa_w_gate, i, nsa_b_gate[i])
            gates = gates.reshape(s, 3, ng, r).transpose(2, 0, 1, 3).reshape(ng, s, 3 * r)
            o = _nsa_attention(q, *kv, gates)
            xs = _mm_resid(o, nsa_w_o, i, zeros_d, m[2], xs)
        h = _prenorm(xs, norm_ffn[layer], m[3], m[4])
        a, w_down = _mm_swiglu(h, ffn_w_gu, ffn_w_down, layer)
        xs = _mm_resid(a, w_down, None, zeros_d, m[5], xs)
    return xs[None]
```

```python
import functools
import math

import numpy as np
import jax
import jax.numpy as jnp
from jax import lax
from jax.experimental import pallas as pl
from jax.experimental.pallas import tpu as pltpu

HEAD_DIM = 128
N_KV_GROUPS = 4
CMP_LEN = 32
CMP_STRIDE = 16
SLC_BLOCK = 64
SLC_TOPK = 16
N_INIT_BLOCKS = 1
N_LOCAL_BLOCKS = 2
WINDOW = 512
ROPE_THETA = 10000.0
EPS = 1e-6
NEG_INF = -1e30
LOG2_E = math.log2(math.e)

LANES = 128
SUBLANES = 8
VMEM_LIMIT_BYTES = 56 * 1024 * 1024

BF16 = jnp.bfloat16
F32 = jnp.float32


def _params(*sem):
    return pltpu.CompilerParams(dimension_semantics=sem, vmem_limit_bytes=VMEM_LIMIT_BYTES)


def _tile(n, target, quantum):
    if n <= target:
        return n
    t = (target // quantum) * quantum
    while t >= quantum:
        if n % t == 0:
            return t
        t -= quantum
    raise ValueError(f"no tile for {n} with quantum {quantum}")


def _round_up(n, q):
    return -(-n // q) * q


def _dot(a, b):
    return jnp.dot(a, b, preferred_element_type=F32)


def _dot_nt(a, b):
    return lax.dot_general(a, b, (((1,), (1,)), ((), ())), preferred_element_type=F32)


def _sigmoid(v):
    return 1.0 / (1.0 + jnp.exp(-v))


def _silu(v):
    return v * _sigmoid(v)


def _lane_tile(v, width):
    reps = width // v.shape[1]
    return v if reps == 1 else jnp.concatenate([v] * reps, axis=1)


def _ada_kernel(cb_ref, w_ref, b_ref, o_ref):
    d, tn = w_ref.shape
    n_chunks = tn // LANES

    def body(k, accs):
        r = pl.multiple_of(k * SUBLANES, SUBLANES)
        cv = _silu(cb_ref[pl.ds(r, SUBLANES), :])
        return tuple(
            accs[j] + cv * w_ref[pl.ds(r, SUBLANES), j * LANES:(j + 1) * LANES]
            for j in range(n_chunks))

    accs = lax.fori_loop(0, d // SUBLANES, body,
                         tuple(jnp.zeros((SUBLANES, LANES), F32) for _ in range(n_chunks)), unroll=4)
    for j in range(n_chunks):
        o_ref[:, j * LANES:(j + 1) * LANES] = (
            jnp.sum(accs[j], axis=0, keepdims=True) + b_ref[:, j * LANES:(j + 1) * LANES])


def _ada_proj(c, w_ada, b_ada):
    d, n = w_ada.shape
    tn = _tile(n, 1024, LANES)
    cb = jnp.broadcast_to(c.reshape(d, 1), (d, LANES))
    return pl.pallas_call(
        _ada_kernel,
        grid=(n // tn,),
        in_specs=[pl.BlockSpec((d, LANES), lambda j: (0, 0)),
                  pl.BlockSpec((d, tn), lambda j: (0, j)),
                  pl.BlockSpec((1, tn), lambda j: (0, j))],
        out_specs=pl.BlockSpec((1, tn), lambda j: (0, j)),
        out_shape=jax.ShapeDtypeStruct((1, n), F32),
        compiler_params=_params("arbitrary"),
        name="ada_proj",
    )(cb, w_ada, b_ada.reshape(1, n))


def _rope_table_kernel(pos_ref, inv_ref, cos_ref, sin_ref):
    ang = pos_ref[...].astype(F32) * inv_ref[...]
    lane = lax.broadcasted_iota(jnp.int32, ang.shape, 1)
    cos_ref[...] = jnp.cos(ang)
    s = jnp.sin(ang)
    sin_ref[...] = jnp.where(lane < HEAD_DIM // 2, -s, s)


def _rope_tables(pos):
    n = pos.shape[0]
    half = HEAD_DIM // 2
    inv = ROPE_THETA ** (-jnp.arange(half, dtype=F32) / half)
    inv2 = jnp.concatenate([inv, inv]).reshape(1, HEAD_DIM)
    tm = _tile(n, 1024, SUBLANES)
    return pl.pallas_call(
        _rope_table_kernel,
        grid=(n // tm,),
        in_specs=[pl.BlockSpec((tm, 1), lambda i: (i, 0)),
                  pl.BlockSpec((1, HEAD_DIM), lambda i: (0, 0))],
        out_specs=[pl.BlockSpec((tm, HEAD_DIM), lambda i: (i, 0)),
                   pl.BlockSpec((tm, HEAD_DIM), lambda i: (i, 0))],
        out_shape=[jax.ShapeDtypeStruct((n, HEAD_DIM), F32)] * 2,
        compiler_params=_params("arbitrary"),
        name="rope_tables",
    )(pos.reshape(n, 1), inv2)


def _rms_rope(v, g, cos, sin):
    vg = v * g
    rotated = vg * cos + pltpu.roll(vg, HEAD_DIM // 2, axis=1) * sin
    sq = v * v
    hi = sq.astype(BF16)
    lo = (sq - hi.astype(F32)).astype(BF16)
    ones = jnp.ones((HEAD_DIM, HEAD_DIM), BF16)
    sum_sq = _dot(hi, ones) + _dot(lo, ones)
    return rotated * lax.rsqrt(sum_sq * (1.0 / HEAD_DIM) + EPS)


def _prenorm_kernel(x_ref, g_ref, shift_ref, scale_ref, o_ref):
    x = x_ref[...]
    y = x * lax.rsqrt(jnp.mean(x * x, axis=-1, keepdims=True) + EPS) * g_ref[...]
    o_ref[...] = (y * (1.0 + scale_ref[...]) + shift_ref[...]).astype(o_ref.dtype)


def _prenorm(x, g, shift, scale):
    s, d = x.shape
    tm = _tile(s, 512, SUBLANES)
    vec = pl.BlockSpec((1, d), lambda i: (0, 0))
    return pl.pallas_call(
        _prenorm_kernel,
        grid=(s // tm,),
        in_specs=[pl.BlockSpec((tm, d), lambda i: (i, 0)), vec, vec, vec],
        out_specs=pl.BlockSpec((tm, d), lambda i: (i, 0)),
        out_shape=jax.ShapeDtypeStruct((s, d), BF16),
        compiler_params=_params("arbitrary"),
        name="prenorm",
    )(x, g.reshape(1, d), shift.reshape(1, d), scale.reshape(1, d))


def _wblock(w_ref):
    w = w_ref[...]
    return w if w.dtype == BF16 else w.astype(BF16)


def _wspec(w, tn, layer, col_block):
    return pl.BlockSpec((pl.Squeezed(), w.shape[1], tn), lambda i, j: (layer, 0, col_block(j)))


def _mm_glu_kernel(a_ref, w1_ref, w2_ref, b1_ref, b2_ref, o_ref):
    a = a_ref[...]
    lin = _dot(a, _wblock(w1_ref)) + b1_ref[...]
    gate = _dot(a, _wblock(w2_ref)) + b2_ref[...]
    o_ref[...] = (lin * _sigmoid(gate)).astype(o_ref.dtype)


def _mm_glu(a, w, layer, b):
    m, k = a.shape
    n = w.shape[2] // 2
    tm = _tile(m, 1024, SUBLANES)
    tn = _tile(n, 256, LANES)
    nj = n // tn
    b2d = b.reshape(1, 2 * n)
    return pl.pallas_call(
        _mm_glu_kernel,
        grid=(m // tm, nj),
        in_specs=[pl.BlockSpec((tm, k), lambda i, j: (i, 0)),
                  _wspec(w, tn, layer, lambda j: j),
                  _wspec(w, tn, layer, lambda j: j + nj),
                  pl.BlockSpec((1, tn), lambda i, j: (0, j)),
                  pl.BlockSpec((1, tn), lambda i, j: (0, j + nj))],
        out_specs=pl.BlockSpec((tm, tn), lambda i, j: (i, j)),
        out_shape=jax.ShapeDtypeStruct((m, n), F32),
        compiler_params=_params("arbitrary", "arbitrary"),
        name="mm_glu",
    )(a, w, w, b2d, b2d)


DOWN_TN = 256


def _mm_swiglu_kernel(a_ref, wg_ref, wu_ref, wd_ref, o_ref, wd_out_ref):
    a = a_ref[...]
    o_ref[...] = (_silu(_dot(a, _wblock(wg_ref))) * _dot(a, _wblock(wu_ref))).astype(o_ref.dtype)
    wd = wd_ref[...]
    tn = wd_out_ref.shape[2]
    for jb in range(wd_out_ref.shape[0]):
        wd_out_ref[jb] = wd[:, jb * tn:(jb + 1) * tn].astype(wd_out_ref.dtype)


def _mm_swiglu(a, w, w_down, layer):
    m, k = a.shape
    n = w.shape[2] // 2
    d = w_down.shape[2]
    tm = _tile(m, 1024, SUBLANES)
    tn = _tile(n, 256, LANES)
    nj = n // tn
    dn = _tile(d, DOWN_TN, LANES)
    slab, rem = divmod(n, (m // tm) * nj)
    assert rem == 0 and slab % (2 * SUBLANES) == 0, "w_down rows must split evenly over the grid steps"
    return pl.pallas_call(
        _mm_swiglu_kernel,
        grid=(m // tm, nj),
        in_specs=[pl.BlockSpec((tm, k), lambda i, j: (i, 0)),
                  _wspec(w, tn, layer, lambda j: j),
                  _wspec(w, tn, layer, lambda j: j + nj),
                  pl.BlockSpec((pl.Squeezed(), slab, d), lambda i, j: (layer, i * nj + j, 0))],
        out_specs=[pl.BlockSpec((tm, tn), lambda i, j: (i, j)),
                   pl.BlockSpec((d // dn, slab, dn), lambda i, j: (0, i * nj + j, 0))],
        out_shape=[jax.ShapeDtypeStruct((m, n), BF16), jax.ShapeDtypeStruct((d // dn, n, dn), BF16)],
        compiler_params=_params("arbitrary", "arbitrary"),
        name="mm_swiglu",
    )(a, w, w, w_down)


def _mm_resid_kernel(a_ref, w_ref, bias_ref, gate_ref, x_ref, o_ref):
    y = _dot(a_ref[...], _wblock(w_ref)) + bias_ref[...]
    o_ref[...] = x_ref[...] + gate_ref[...] * y


def _mm_resid(a, w, layer, bias, gate, x):
    m, k = a.shape
    if layer is None:
        tn = w.shape[2]
        n = w.shape[0] * tn
        w_spec = pl.BlockSpec((pl.Squeezed(), k, tn), lambda i, j: (j, 0, 0))
    else:
        n = w.shape[2]
        tn = _tile(n, 512, LANES)
        w_spec = _wspec(w, tn, layer, lambda j: j)
    tm = _tile(m, 512 if k > 8192 else 1024, SUBLANES)
    vec = pl.BlockSpec((1, tn), lambda i, j: (0, j))
    return pl.pallas_call(
        _mm_resid_kernel,
        grid=(m // tm, n // tn),
        in_specs=[pl.BlockSpec((tm, k), lambda i, j: (i, 0)),
                  w_spec,
                  vec, vec,
                  pl.BlockSpec((tm, tn), lambda i, j: (i, j))],
        out_specs=pl.BlockSpec((tm, tn), lambda i, j: (i, j)),
        out_shape=jax.ShapeDtypeStruct((m, n), F32),
        compiler_params=_params("arbitrary", "arbitrary"),
        name="mm_resid",
    )(a, w, bias.reshape(1, n), gate.reshape(1, n), x)


def _mm_sigmoid_kernel(a_ref, w_ref, b_ref, o_ref):
    o_ref[...] = _sigmoid(_dot(a_ref[...], _wblock(w_ref)) + b_ref[...])


def _mm_sigmoid(a, w, layer, b):
    m, k = a.shape
    n = w.shape[2]
    tm = _tile(m, 1024, SUBLANES)
    return pl.pallas_call(
        _mm_sigmoid_kernel,
        grid=(m // tm, 1),
        in_specs=[pl.BlockSpec((tm, k), lambda i, j: (i, 0)),
                  _wspec(w, n, layer, lambda j: 0),
                  pl.BlockSpec((1, n), lambda i, j: (0, 0))],
        out_specs=pl.BlockSpec((tm, n), lambda i, j: (i, 0)),
        out_shape=jax.ShapeDtypeStruct((m, n), F32),
        compiler_params=_params("arbitrary", "arbitrary"),
        name="mm_sigmoid",
    )(a, w, b.reshape(1, n))


HEADS_CHUNK = 128


def _mm_heads_kernel(a_ref, w_ref, g_ref, cos_ref, sin_ref, o_ref, y_ref, *, rope_every, out_scale):
    j = pl.program_id(1)
    nh, tm, _ = o_ref.shape
    y = _dot(a_ref[...], _wblock(w_ref))

    def plain():
        for h in range(nh):
            o_ref[h] = y[:, h * HEAD_DIM:(h + 1) * HEAD_DIM].astype(o_ref.dtype)

    def roped():
        y_ref[...] = y
        g = g_ref[0]
        chunk = min(HEADS_CHUNK, tm)

        def body(c, carry):
            rs = pl.ds(pl.multiple_of(c * chunk, chunk), chunk)
            cos, sin = cos_ref[rs, :], sin_ref[rs, :]
            for h in range(nh):
                v = _rms_rope(y_ref[rs, h * HEAD_DIM:(h + 1) * HEAD_DIM], g, cos, sin)
                o_ref[h, rs, :] = (v * out_scale).astype(o_ref.dtype)
            return carry

        lax.fori_loop(0, tm // chunk, body, 0, unroll=2)

    if rope_every == 0:
        plain()
    elif rope_every == 1:
        roped()
    else:
        pl.when(j % rope_every == 0)(roped)
        pl.when(j % rope_every != 0)(plain)


def _mm_heads(a, w, layer, col0, n_cols, gains, gain_of_block, cos, sin, *, rope_every, out_scale, out_dtype):
    m, k = a.shape
    tn = 4 * HEAD_DIM
    nh = tn // HEAD_DIM
    tm = _tile(m, 1024, SUBLANES)
    j0 = col0 // tn
    ng = gains.shape[0]
    return pl.pallas_call(
        functools.partial(_mm_heads_kernel, rope_every=rope_every, out_scale=out_scale),
        grid=(m // tm, n_cols // tn),
        in_specs=[pl.BlockSpec((tm, k), lambda i, j: (i, 0)),
                  _wspec(w, tn, layer, lambda j: j + j0),
                  pl.BlockSpec((1, 1, HEAD_DIM), lambda i, j: (gain_of_block(j), 0, 0)),
                  pl.BlockSpec((tm, HEAD_DIM), lambda i, j: (i, 0)),
                  pl.BlockSpec((tm, HEAD_DIM), lambda i, j: (i, 0))],
        out_specs=pl.BlockSpec((nh, tm, HEAD_DIM), lambda i, j: (j, i, 0)),
        out_shape=jax.ShapeDtypeStruct((n_cols // HEAD_DIM, m, HEAD_DIM), out_dtype),
        scratch_shapes=[pltpu.VMEM((tm, tn), F32)],
        compiler_params=_params("arbitrary", "arbitrary"),
        name="mm_heads",
    )(a, w, gains.reshape(ng, 1, HEAD_DIM), cos, sin)


CONV_ROWS = 128
CONV_HALO = 32


def _conv_kernel(halo_ref, u_ref, wdw_ref, bdw_ref, g_ref, b_ref, o_ref, buf_ref, acc_ref):
    i = pl.program_id(0)
    tm, d = u_ref.shape
    cw = wdw_ref.shape[0]
    buf_ref[pl.ds(CONV_HALO, tm), :] = u_ref[...]
    buf_ref[pl.ds(0, CONV_HALO), :] = jnp.where(i > 0, halo_ref[...], 0.0)
    off = CONV_HALO - (cw - 1)

    def col_body(c, carry):
        c0 = pl.multiple_of(c * LANES, LANES)
        acc = jnp.zeros((tm, LANES), F32)
        window = buf_ref[:, pl.ds(c0, LANES)]
        n_buf = window.shape[0]
        for b in range(SUBLANES):
            taps = [(a, SUBLANES * a + b - off) for a in range((off + cw - 1) // SUBLANES + 1)
                    if 0 <= SUBLANES * a + b - off < cw]
            shifted = window if b == 0 else pltpu.roll(window, n_buf - b, axis=0)
            for a, w in taps:
                acc = acc + shifted[SUBLANES * a:SUBLANES * a + tm, :] * wdw_ref[pl.ds(w, 1), pl.ds(c0, LANES)]
        acc_ref[:, pl.ds(c0, LANES)] = acc + bdw_ref[:, pl.ds(c0, LANES)]
        return carry

    lax.fori_loop(0, d // LANES, col_body, 0)
    y = acc_ref[...]
    mu = jnp.mean(y, axis=-1, keepdims=True)
    yc = y - mu
    var = jnp.mean(yc * yc, axis=-1, keepdims=True)
    z = yc * lax.rsqrt(var + EPS) * g_ref[...] + b_ref[...]
    o_ref[...] = _silu(z).astype(o_ref.dtype)


def _conv_ln_silu(u, w_dw, b_dw, ln_g, ln_b):
    s, d = u.shape
    cw = w_dw.shape[0]
    assert cw - 1 <= CONV_HALO
    tm = _tile(s, CONV_ROWS, CONV_HALO)
    ratio = tm // CONV_HALO
    vec = pl.BlockSpec((1, d), lambda i: (0, 0))
    return pl.pallas_call(
        _conv_kernel,
        grid=(s // tm,),
        in_specs=[pl.BlockSpec((CONV_HALO, d), lambda i: (jnp.maximum(i * ratio - 1, 0), 0)),
                  pl.BlockSpec((tm, d), lambda i: (i, 0)),
                  pl.BlockSpec((cw, d), lambda i: (0, 0)),
                  vec, vec, vec],
        out_specs=pl.BlockSpec((tm, d), lambda i: (i, 0)),
        out_shape=jax.ShapeDtypeStruct((s, d), BF16),
        scratch_shapes=[pltpu.VMEM((CONV_HALO + tm, d), F32), pltpu.VMEM((tm, d), F32)],
        compiler_params=_params("arbitrary"),
        name="conv_ln_silu",
    )(u, u, w_dw, b_dw.reshape(1, d), ln_g.reshape(1, d), ln_b.reshape(1, d))


def _compress_kernel(c_ref, p_ref, w1_ref, b1_ref, w2_ref, b2_ref, g_ref, cos_ref, sin_ref, o_ref):
    j = pl.program_id(0)
    nch, half = c_ref.shape
    c = c_ref[...]
    a_lo = (c + p_ref[0:1, :]).astype(BF16)
    a_hi = (c + p_ref[1:2, :]).astype(BF16)
    h_lo = _dot(a_lo, w1_ref[pl.ds(0, half), :])
    h_hi = _dot(a_hi, w1_ref[pl.ds(half, half), :])
    hid = h_lo + pltpu.roll(h_hi, nch - 1, axis=0) + b1_ref[...]
    y = _dot(_silu(hid).astype(BF16), w2_ref[...]) + b2_ref[...]

    @pl.when(j == 0)
    def _():
        o_ref[...] = _rms_rope(y, g_ref[...], cos_ref[...], sin_ref[...]).astype(o_ref.dtype)

    @pl.when(j != 0)
    def _():
        o_ref[...] = y.astype(o_ref.dtype)


def _compress(raw, cmp_pos, w1, b1, w2, b2, g, cos_c, sin_c):
    _, ng, s, dh = raw.shape
    nch = s // CMP_STRIDE
    half = CMP_STRIDE * dh
    hid = w1.shape[-1]
    chunks = raw.reshape(2, ng, nch, half)
    pos2 = cmp_pos.reshape(2, CMP_LEN // CMP_STRIDE, half)
    sq = pl.Squeezed()
    return pl.pallas_call(
        _compress_kernel,
        grid=(2, ng),
        in_specs=[pl.BlockSpec((sq, sq, nch, half), lambda j, gi: (j, gi, 0, 0)),
                  pl.BlockSpec((sq, CMP_LEN // CMP_STRIDE, half), lambda j, gi: (j, 0, 0)),
                  pl.BlockSpec((sq, 2 * half, hid), lambda j, gi: (j, 0, 0)),
                  pl.BlockSpec((sq, 1, hid), lambda j, gi: (j, 0, 0)),
                  pl.BlockSpec((sq, hid, dh), lambda j, gi: (j, 0, 0)),
                  pl.BlockSpec((sq, 1, dh), lambda j, gi: (j, 0, 0)),
                  pl.BlockSpec((1, dh), lambda j, gi: (0, 0)),
                  pl.BlockSpec((nch, dh), lambda j, gi: (0, 0)),
                  pl.BlockSpec((nch, dh), lambda j, gi: (0, 0))],
        out_specs=pl.BlockSpec((sq, sq, nch, dh), lambda j, gi: (j, gi, 0, 0)),
        out_shape=jax.ShapeDtypeStruct((2, ng, nch, dh), BF16),
        compiler_params=_params("arbitrary", "arbitrary"),
        name="compress",
    )(chunks, pos2, w1, b1.reshape(2, 1, hid), w2, b2.reshape(2, 1, dh), g.reshape(1, dh), cos_c, sin_c)


NSA_TQ = 128
NSA_TK = 512


def _nsa_kernel(q_ref, kc_ref, vc_ref, ksa_ref, vsa_ref, kw_ref, vw_ref, gate_ref, mt_ref, o_ref,
                cnt_ref, qa_ref, sa_ref, sb_ref, mxa_ref, mxb_ref, pc_ref, pw_ref, m_ref, acc_ref, out_ref):
    r, tq, dh = q_ref.shape
    rows = r * tq
    nc = kc_ref.shape[0]
    ns = mt_ref.shape[0]
    s_len = kw_ref.shape[0]
    n_aug = ksa_ref.shape[1] - dh
    tk = min(NSA_TK, s_len)
    wk = min(WINDOW + tq, s_len)
    qt = pl.program_id(1)
    t0 = qt * tq
    q2 = q_ref[...].reshape(rows, dh)
    t_q = t0 + lax.broadcasted_iota(jnp.int32, (tq, 1), 0)
    gates = gate_ref[...]
    heads = [slice(h * tq, (h + 1) * tq) for h in range(r)]

    k_start = pl.multiple_of(jnp.maximum(jnp.minimum(t0 + tq, s_len) - wk, 0), SUBLANES * 2)
    sa_ref[:, :nc] = _dot_nt(q2, kc_ref[...])
    sb_ref[:, :wk] = _dot_nt(q2, kw_ref[pl.ds(k_start, wk), :])

    cmp_end = lax.broadcasted_iota(jnp.int32, (1, nc), 1) * CMP_STRIDE + (CMP_LEN - 1)
    bias_c = jnp.where(cmp_end <= t_q, 0.0, NEG_INF)
    has_block = t_q >= CMP_LEN - 1
    p_sum = jnp.zeros((tq, nc), F32)
    for rs in heads:
        s = sa_ref[rs, :nc] + bias_c
        e = jnp.exp2(s - jnp.max(s, axis=-1, keepdims=True))
        l = jnp.sum(e, axis=-1, keepdims=True)
        p = e * jnp.where(has_block, 1.0 / l, 0.0)
        p_sum = p_sum + p
        pc_ref[rs, :] = p.astype(BF16)

    back = t_q - (k_start + lax.broadcasted_iota(jnp.int32, (1, wk), 1))
    bias_w = jnp.where((back >= 0) & (back < WINDOW), 0.0, NEG_INF)
    for rs in heads:
        s = sb_ref[rs, :wk] + bias_w
        pw_ref[rs, :] = jnp.exp2(s - jnp.max(s, axis=-1, keepdims=True)).astype(BF16)

    o_c = _dot(pc_ref[...], vc_ref[...])
    o_w = _dot(pw_ref[...], vw_ref[pl.ds(k_start, wk), :])
    for h, rs in enumerate(heads):
        out_ref[rs, :] = (gates[:, h:h + 1] * o_c[rs, :]
                          + gates[:, 2 * r + h:2 * r + h + 1] * (o_w[rs, :dh] / o_w[rs, dh:2 * dh]))

    p_hi = p_sum.astype(BF16)
    rem = p_sum - p_hi.astype(F32)
    p_mid = rem.astype(BF16)
    p_lo = (rem - p_mid.astype(F32)).astype(BF16)
    mt = mt_ref[...]
    imp = _dot_nt(mt, p_hi) + _dot_nt(mt, p_mid) + _dot_nt(mt, p_lo)

    t_row = t0 + lax.broadcasted_iota(jnp.int32, (1, tq), 1)
    blk = lax.broadcasted_iota(jnp.int32, (ns, 1), 0)
    dist = t_row // SLC_BLOCK - blk
    forced = (blk < N_INIT_BLOCKS) | ((dist >= 0) & (dist < N_LOCAL_BLOCKS))
    imp = jnp.where(forced, jnp.inf, jnp.where(dist >= 0, imp, -jnp.inf))

    n_rivals = jnp.minimum((t0 + tq - 1) // SLC_BLOCK + 1, ns)
    cnt_ref[...] = jnp.zeros(cnt_ref.shape, F32)
    for c in range(ns // SUBLANES):
        lo = c * SUBLANES

        @pl.when(lo < n_rivals)
        def _():
            below, own, above = imp[:lo], imp[lo:lo + SUBLANES], imp[lo + SUBLANES:]
            own_idx = lo + lax.broadcasted_iota(jnp.int32, (SUBLANES, 1), 0)
            cnt = cnt_ref[...]
            for i in range(lo, lo + SUBLANES):
                row = imp[i:i + 1, :]
                parts = [jnp.where(row > below, 1.0, 0.0),
                         jnp.where((row > own) | ((row == own) & (i < own_idx)), 1.0, 0.0),
                         jnp.where(row >= above, 1.0, 0.0)]
                cnt = cnt + jnp.concatenate([p for p in parts if p.shape[0]], axis=0)
            cnt_ref[...] = cnt

    chosen = (cnt_ref[...] < float(min(SLC_TOPK, ns))) & (dist >= 0)
    not_sel = jnp.where(chosen, 0.0, 1.0).T.astype(BF16)
    if n_aug > ns:
        not_sel = jnp.concatenate([not_sel, jnp.zeros((tq, n_aug - ns), BF16)], axis=1)
    qa_ref[:, :dh] = q2
    for rs in heads:
        qa_ref[rs, dh:] = not_sel

    m_ref[...] = jnp.full(m_ref.shape, NEG_INF, F32)
    acc_ref[...] = jnp.zeros(acc_ref.shape, F32)
    n_tiles = (t0 + tq + tk - 1) // tk
    last_tile = s_len // tk - 1

    def scores(kt, s_ref, mx_ref):
        k0 = pl.multiple_of(jnp.minimum(kt, last_tile) * tk, tk)
        kpos = kt * tk + lax.broadcasted_iota(jnp.int32, (1, tk), 1)
        bias = jnp.where(kpos <= t_q, 0.0, NEG_INF)
        raw = _dot_nt(qa_ref[...], ksa_ref[pl.ds(k0, tk), :])
        for rs in heads:
            s = raw[rs, :] + bias
            s_ref[rs, :tk] = s
            mx_ref[rs, :] = jnp.broadcast_to(jnp.max(s, axis=-1, keepdims=True), (tq, LANES))

    def absorb(kt, s_ref, mx_ref):
        probs, alphas = [], []
        for rs in heads:
            m_old = m_ref[rs, :]
            m_new = jnp.maximum(m_old, mx_ref[rs, :])
            probs.append(jnp.exp2(s_ref[rs, :tk] - _lane_tile(m_new, tk)).astype(BF16))
            alphas.append(jnp.exp2(m_old - m_new))
            m_ref[rs, :] = m_new
        k0 = pl.multiple_of(jnp.minimum(kt, last_tile) * tk, tk)
        pv = _dot(jnp.concatenate(probs, axis=0), vsa_ref[pl.ds(k0, tk), :])
        alpha = jnp.concatenate(alphas, axis=0)
        acc_ref[...] = acc_ref[...] * _lane_tile(alpha, acc_ref.shape[1]) + pv

    scores(0, sa_ref, mxa_ref)

    def pair_body(j, carry):
        kt = 2 * j
        scores(kt + 1, sb_ref, mxb_ref)
        absorb(kt, sa_ref, mxa_ref)
        scores(kt + 2, sa_ref, mxa_ref)
        absorb(kt + 1, sb_ref, mxb_ref)
        return carry

    lax.fori_loop(0, n_tiles // 2, pair_body, 0)

    @pl.when(n_tiles % 2 == 1)
    def _():
        absorb(n_tiles - 1, sa_ref, mxa_ref)

    for h, rs in enumerate(heads):
        o_s = acc_ref[rs, :dh] / acc_ref[rs, dh:2 * dh]
        o_ref[:, h * dh:(h + 1) * dh] = (out_ref[rs, :] + gates[:, r + h:r + h + 1] * o_s).astype(o_ref.dtype)


def _cmp_to_slc_t(n_chunks, n_slc):
    n_cmp = n_chunks - 1
    r, cl = SLC_BLOCK // CMP_STRIDE, CMP_LEN // CMP_STRIDE
    offs = (np.arange(r)[:, None] - np.arange(cl)[None, :]).reshape(-1)
    tgt = r * np.arange(n_slc)[None, :, None] + offs[None, None, :]
    m = (np.arange(n_cmp)[:, None, None] == tgt).sum(-1).astype(np.float32)
    out = np.zeros((n_slc, n_chunks), np.float32)
    out[:, :n_cmp] = m.T
    return out


def _with_block_markers(k_slc):
    ng, s, _ = k_slc.shape
    n_aug = _round_up(s // SLC_BLOCK, LANES)
    own_block = (np.arange(s)[:, None] // SLC_BLOCK) == np.arange(n_aug)[None, :]
    marker = jnp.asarray(np.where(own_block, NEG_INF, 0.0), BF16)
    return jnp.concatenate([k_slc, jnp.broadcast_to(marker, (ng, s, n_aug))], axis=-1)


def _with_ones(v):
    return jnp.concatenate([v, jnp.ones_like(v)], axis=-1)


def _nsa_attention(q, k_cmp, v_cmp, k_aug, v_aug, k_win, v_win, gates):
    n_heads, s, dh = q.shape
    ng = k_win.shape[0]
    r = n_heads // ng
    nc = k_cmp.shape[1]
    ns = s // SLC_BLOCK
    tq = _tile(s, NSA_TQ, SLC_BLOCK)
    rows = r * tq
    tk = min(NSA_TK, s)
    wk = min(WINDOW + tq, s)
    width = max(nc, tk, wk)
    mt = jnp.asarray(_cmp_to_slc_t(nc, ns), BF16)
    sq = pl.Squeezed()
    per_group = lambda arr: pl.BlockSpec((sq,) + arr.shape[1:], lambda gi, qi: (gi, 0, 0))
    return pl.pallas_call(
        _nsa_kernel,
        grid=(ng, s // tq),
        in_specs=[pl.BlockSpec((r, tq, dh), lambda gi, qi: (gi, qi, 0)),
                  per_group(k_cmp), per_group(v_cmp), per_group(k_aug), per_group(v_aug),
                  per_group(k_win), per_group(v_win),
                  pl.BlockSpec((sq, tq, 3 * r), lambda gi, qi: (gi, qi, 0)),
                  pl.BlockSpec((ns, nc), lambda gi, qi: (0, 0))],
        out_specs=pl.BlockSpec((tq, r * dh), lambda gi, qi: (qi, gi)),
        out_shape=jax.ShapeDtypeStruct((s, n_heads * dh), BF16),
        scratch_shapes=[pltpu.VMEM((ns, tq), F32),
                        pltpu.VMEM((rows, k_aug.shape[2]), BF16),
                        pltpu.VMEM((rows, width), F32),
                        pltpu.VMEM((rows, width), F32),
                        pltpu.VMEM((rows, LANES), F32),
                        pltpu.VMEM((rows, LANES), F32),
                        pltpu.VMEM((rows, nc), BF16),
                        pltpu.VMEM((rows, wk), BF16),
                        pltpu.VMEM((rows, LANES), F32),
                        pltpu.VMEM((rows, v_aug.shape[2]), F32),
                        pltpu.VMEM((rows, dh), F32)],
        compiler_params=_params("arbitrary", "arbitrary"),
        name="nsa_attention",
    )(q, k_cmp, v_cmp, k_aug, v_aug, k_win, v_win, gates, mt)


def kernel(x, c, positions, w_ada, b_ada, ada_emb, kv_ada_emb, norm_mix, norm_ffn, norm_kv, conv_w_pw1, conv_b_pw1, conv_w_dw, conv_b_dw, conv_ln_g, conv_ln_b, conv_w_pw2, conv_b_pw2, nsa_w_q, nsa_q_norm, nsa_w_gate, nsa_b_gate, nsa_w_o, kv_w, kv_k_norm, cmp_pos, cmp_w1, cmp_b1, cmp_w2, cmp_b2, ffn_w_gu, ffn_w_down):
    batch, s, d = x.shape
    assert batch == 1, "the kernels are written for a single sequence"
    depth = ada_emb.shape[0]
    n_conv = conv_w_pw1.shape[0]
    ng = N_KV_GROUPS
    n_heads = d // HEAD_DIM
    r = n_heads // ng
    kvw = ng * HEAD_DIM
    zeros_d = jnp.zeros((d,), F32)

    xs = x[0]
    mod = _ada_proj(c, w_ada, b_ada).reshape(6, d)
    pos = positions[0]
    cos, sin = _rope_tables(pos)
    kv = None
    for layer in range(depth):
        m = mod + ada_emb[layer]
        h = _prenorm(xs, norm_mix[layer], m[0], m[1])
        if layer < n_conv:
            u = _mm_glu(h, conv_w_pw1, layer, conv_b_pw1[layer])
            v = _conv_ln_silu(u, conv_w_dw[layer], conv_b_dw[layer], conv_ln_g[layer], conv_ln_b[layer])
            xs = _mm_resid(v, conv_w_pw2, layer, conv_b_pw2[layer], m[2], xs)
        else:
            if kv is None:
                h_kv = _prenorm(xs, norm_kv, mod[0] + kv_ada_emb[0], mod[1] + kv_ada_emb[1])
                kv_w3 = kv_w[None]
                raw = _mm_heads(h_kv, kv_w3, 0, 0, 2 * kvw, kv_k_norm, lambda j: 0, cos, sin,
                                rope_every=0, out_scale=1.0, out_dtype=F32)
                rest = _mm_heads(h_kv, kv_w3, 0, 2 * kvw, 4 * kvw, kv_k_norm, lambda j: 1 + j // 2, cos, sin,
                                 rope_every=2, out_scale=1.0, out_dtype=BF16)
                rest = rest.reshape(4, ng, s, HEAD_DIM)
                n_chunks = s // CMP_STRIDE
                pos_c = jnp.pad(pos[CMP_LEN - 1::CMP_STRIDE], (0, 1))[:n_chunks]
                cos_c, sin_c = _rope_tables(pos_c)
                cmp = _compress(raw.reshape(2, ng, s, HEAD_DIM), cmp_pos, cmp_w1.astype(BF16), cmp_b1,
                                cmp_w2.astype(BF16), cmp_b2, kv_k_norm[0], cos_c, sin_c)
                kv = (cmp[0], cmp[1], _with_block_markers(rest[0]), _with_ones(rest[1]),
                      rest[2], _with_ones(rest[3]))
            i = layer - n_conv
            q = _mm_heads(h, nsa_w_q, i, 0, n_heads * HEAD_DIM, nsa_q_norm[i:i + 1],
                          lambda j: 0, cos, sin, rope_every=1, out_scale=HEAD_DIM ** -0.5 * LOG2_E,
                          out_dtype=BF16)
            gates = _mm_sigmoid(h, nsa_w_gate, i, nsa_b_gate[i])
            gates = gates.reshape(s, 3, ng, r).transpose(2, 0, 1, 3).reshape(ng, s, 3 * r)
            o = _nsa_attention(q, *kv, gates)
            xs = _mm_resid(o, nsa_w_o, i, zeros_d, m[2], xs)
        h = _prenorm(xs, norm_ffn[layer], m[3], m[4])
        a, w_down = _mm_swiglu(h, ffn_w_gu, ffn_w_down, layer)
        xs = _mm_resid(a, w_down, None, zeros_d, m[5], xs)
    return xs[None]
```
